```python
import math
import jax
import jax.numpy as jnp
from jax import lax
import numpy as np

D_MODEL = 1024
BATCH = 16
SEQ = 256
DEPTH = 4
DEC_BATCH = 2
DEC_SEQ = 4096
PAST_LEN = 256

GRID_W = 64
CONV_W = 256
CONV_K = 3
NA_HEADS = 6
NA_HEAD_DIM = 64
NA_W = NA_HEADS * NA_HEAD_DIM
NA_WIN_R = 8
NA_WIN_C = 16
MLA_HEADS = 6
MLA_NOPE_DIM = 64
MLA_ROPE_DIM = 32
MLA_V_DIM = 64
MLA_W = MLA_HEADS * MLA_V_DIM
MLA_Q_RANK = 256
MLA_KV_RANK = 128
MIX_W = CONV_W + NA_W + MLA_W
IN_COLS = 3 * CONV_W + 3 * NA_W + MLA_Q_RANK + MLA_KV_RANK + MLA_ROPE_DIM
ROPE_BASE = 10000.0
D_FF = 2816
N_EXPERTS = 8
TOP_K = 2
D_FF_EXPERT = 3584
N_DENSE = (DEPTH + 1) // 2
N_MOE = DEPTH // 2
Q_BLOCK = 128
NORM_EPS = 1e-6
NEG_INF = -1e30

kernel_name = "hybrid_dit_conv_natten_mla_step"


def _rms_norm(x, g):
    xf = x.astype(jnp.float32)
    y = xf * lax.rsqrt(jnp.mean(xf * xf, axis=-1, keepdims=True) + NORM_EPS)
    return (y * g.astype(jnp.float32)).astype(x.dtype)


def _adaln(cond, w_mod, b_mod):
    m = jax.nn.silu(cond) @ w_mod + b_mod
    return [p[:, None, :] for p in jnp.split(m, 6, axis=-1)]


def _project(h, w_in):
    z = h @ w_in
    sizes = (CONV_W, CONV_W, CONV_W, NA_W, NA_W, NA_W, MLA_Q_RANK, MLA_KV_RANK, MLA_ROPE_DIM)
    idx = [int(i) for i in np.cumsum(sizes)[:-1]]
    return jnp.split(z, idx, axis=-1)


def _short_conv(b_gate, c_gate, x_in, w_conv):
    u = c_gate * x_in
    L = u.shape[1]
    up = jnp.pad(u, ((0, 0), (1, 1), (0, 0)))
    y = w_conv[0] * up[:, :L] + w_conv[1] * up[:, 1:L + 1] + w_conv[2] * up[:, 2:L + 2]
    return b_gate * y


def _rope_2d(x):
    L = x.shape[1]
    half = MLA_ROPE_DIM // 2
    quarter = half // 2
    t = jnp.arange(L)
    pos = jnp.stack([t // GRID_W, t % GRID_W], axis=-1).astype(jnp.float32)
    inv_freq = ROPE_BASE ** (-jnp.arange(quarter, dtype=jnp.float32) * 2.0 / half)
    ang = pos[:, :, None] * inv_freq
    ang = jnp.concatenate([ang, ang], axis=-1)
    bshape = (L,) + (1,) * (x.ndim - 3) + (2, half)
    cos = jnp.cos(ang).reshape(bshape)
    sin = jnp.sin(ang).reshape(bshape)
    xf = x.astype(jnp.float32).reshape(x.shape[:-1] + (2, half))
    rot = jnp.concatenate([-xf[..., quarter:], xf[..., :quarter]], axis=-1)
    return (xf * cos + rot * sin).reshape(x.shape).astype(x.dtype)


def _block_attention(q, k, v, scale):
    B, L, H, Dk = q.shape
    Dv = v.shape[-1]
    nb = L // Q_BLOCK
    qb = q.reshape(B, nb, Q_BLOCK, H, Dk).swapaxes(0, 1)

    def attend(qi):
        s = jnp.einsum('bqhd,bkhd->bhqk', qi, k).astype(jnp.float32) * scale
        p = jax.nn.softmax(s, axis=-1).astype(v.dtype)
        return jnp.einsum('bhqk,bkhe->bqhe', p, v)

    o = lax.map(attend, qb)
    return o.swapaxes(0, 1).reshape(B, L, H * Dv)


def _neighbourhood_attention(q, k, v, k_ctx, v_ctx, rpb):
    B, L, H, Dh = q.shape
    rows = L // GRID_W
    wr = min(NA_WIN_R, rows)
    r = jnp.arange(rows)
    row_start = jnp.clip(r - wr // 2, 0, rows - wr)
    row_idx = row_start[:, None] + jnp.arange(wr)[None, :]
    cpos = jnp.arange(GRID_W)
    col_start = jnp.clip(cpos - NA_WIN_C // 2, 0, GRID_W - NA_WIN_C)
    col_in = (cpos[None, :] >= col_start[:, None]) & (cpos[None, :] < col_start[:, None] + NA_WIN_C)
    qg = q.reshape(B, rows, GRID_W, H, Dh)
    kb = k.reshape(B, rows, GRID_W, H, Dh)[:, row_idx]
    vb = v.reshape(B, rows, GRID_W, H, Dh)[:, row_idx]
    scale = NA_HEAD_DIM ** -0.5
    s_win = jnp.einsum('brchd,briwhd->brchiw', qg, kb).astype(jnp.float32) * scale
    ri = row_idx - r[:, None] + (NA_WIN_R - 1)
    ci = jnp.clip(cpos[None, :] - cpos[:, None], -(NA_WIN_C - 1), NA_WIN_C - 1) + (NA_WIN_C - 1)
    bias = rpb[ri[:, None, :, None], ci[None, :, None, :]]
    bias = jnp.moveaxis(bias, -1, 2).astype(jnp.float32)
    s_win = jnp.where(col_in[None, None, :, None, None, :], s_win + bias[None], NEG_INF)
    s_ctx = jnp.einsum('brchd,bjhd->brchj', qg, k_ctx).astype(jnp.float32) * scale
    nwin = wr * GRID_W
    s = jnp.concatenate([s_win.reshape(B, rows, GRID_W, H, nwin), s_ctx], axis=-1)
    p = jax.nn.softmax(s, axis=-1).astype(v.dtype)
    p_win = p[..., :nwin].reshape(B, rows, GRID_W, H, wr, GRID_W)
    p_ctx = p[..., nwin:]
    o = jnp.einsum('brchiw,briwhd->brchd', p_win, vb) + jnp.einsum('brchj,bjhd->brchd', p_ctx, v_ctx)
    return o.reshape(B, L, H * Dh)


def _mla_expand(ckv, w_kvb):
    B, L, _ = ckv.shape
    kv = (ckv @ w_kvb).reshape(B, L, MLA_HEADS, MLA_NOPE_DIM + MLA_V_DIM)
    return kv[..., :MLA_NOPE_DIM], kv[..., MLA_NOPE_DIM:]


def _mla_keys(k_nope, k_rope):
    B, L, H, _ = k_nope.shape
    return jnp.concatenate([k_nope, jnp.broadcast_to(k_rope[:, :, None, :], (B, L, H, MLA_ROPE_DIM))], axis=-1)


def _merge(conv_o, na_o, mla_o, g_grp, w_out):
    g_c, g_n, g_m = jnp.split(g_grp, [CONV_W, CONV_W + NA_W])
    y = jnp.concatenate([_rms_norm(conv_o, g_c), _rms_norm(na_o, g_n), _rms_norm(mla_o, g_m)], axis=-1)
    return y @ w_out


def _mixer_context(h, w_in, w_conv, g_qa, w_qb, g_kva, w_kvb, g_grp, w_out):
    B, L, _ = h.shape
    b_g, c_g, x_c, q_na, k_na, v_na, qa, kva, k_r = _project(h, w_in)
    conv_o = _short_conv(b_g, c_g, x_c, w_conv)
    q_na = q_na.reshape(B, L, NA_HEADS, NA_HEAD_DIM)
    k_na = k_na.reshape(B, L, NA_HEADS, NA_HEAD_DIM)
    v_na = v_na.reshape(B, L, NA_HEADS, NA_HEAD_DIM)
    na_o = _block_attention(q_na, k_na, v_na, NA_HEAD_DIM ** -0.5)
    q = (_rms_norm(qa, g_qa) @ w_qb).reshape(B, L, MLA_HEADS, MLA_NOPE_DIM + MLA_ROPE_DIM)
    ckv = _rms_norm(kva, g_kva)
    k_nope, v = _mla_expand(ckv, w_kvb)
    mla_o = _block_attention(q, _mla_keys(k_nope, k_r), v, (MLA_NOPE_DIM + MLA_ROPE_DIM) ** -0.5)
    return _merge(conv_o, na_o, mla_o, g_grp, w_out), k_na, v_na, ckv, k_r


def _mixer_latent(h, c_na_k, c_na_v, c_ckv, c_krope, w_in, w_conv, rpb, g_qa, w_qb, g_kva, w_kvb, g_grp, w_out):
    B, L, _ = h.shape
    b_g, c_g, x_c, q_na, k_na, v_na, qa, kva, k_r = _project(h, w_in)
    conv_o = _short_conv(b_g, c_g, x_c, w_conv)
    q_na = q_na.reshape(B, L, NA_HEADS, NA_HEAD_DIM)
    k_na = k_na.reshape(B, L, NA_HEADS, NA_HEAD_DIM)
    v_na = v_na.reshape(B, L, NA_HEADS, NA_HEAD_DIM)
    na_o = _neighbourhood_attention(q_na, k_na, v_na, c_na_k, c_na_v, rpb)
    q = (_rms_norm(qa, g_qa) @ w_qb).reshape(B, L, MLA_HEADS, MLA_NOPE_DIM + MLA_ROPE_DIM)
    q = jnp.concatenate([q[..., :MLA_NOPE_DIM], _rope_2d(q[..., MLA_NOPE_DIM:])], axis=-1)
    ckv = _rms_norm(kva, g_kva)
    k_nope, v = _mla_expand(ckv, w_kvb)
    kc_nope, vc = _mla_expand(c_ckv, w_kvb)
    keys = jnp.concatenate([_mla_keys(k_nope, _rope_2d(k_r)), _mla_keys(kc_nope, c_krope)], axis=1)
    vals = jnp.concatenate([v, vc], axis=1)
    mla_o = _block_attention(q, keys, vals, (MLA_NOPE_DIM + MLA_ROPE_DIM) ** -0.5)
    return _merge(conv_o, na_o, mla_o, g_grp, w_out)


def _swiglu(x, w1, w3, w2):
    return (jax.nn.silu(x @ w1) * (x @ w3)) @ w2


def _moe_ffn(h, w_router, b_router, w1, w3, w2):
    B, L, D = h.shape
    t = h.reshape(B * L, D)
    logits = (t @ w_router).astype(jnp.float32) + b_router.astype(jnp.float32)
    top_v, top_i = lax.top_k(logits, TOP_K)
    gates = jax.nn.softmax(top_v, axis=-1)
    combine = jnp.sum(jax.nn.one_hot(top_i, N_EXPERTS, dtype=jnp.float32) * gates[..., None], axis=1).astype(h.dtype)
    out = jnp.zeros_like(t)
    for e in range(N_EXPERTS):
        out = out + combine[:, e:e + 1] * _swiglu(t, w1[e], w3[e], w2[e])
    return out.reshape(B, L, D)


def _ffn(h, l, w_ff1, w_ff3, w_ff2, w_router, b_router, moe_w1, moe_w3, moe_w2):
    i = l // 2
    if l % 2 == 0:
        return _swiglu(h, w_ff1[i], w_ff3[i], w_ff2[i])
    return _moe_ffn(h, w_router[i], b_router[i], moe_w1[i], moe_w3[i], moe_w2[i])


def setup_inputs(seed: int = 0) -> dict:
    key = jax.random.key(seed)
    ks = jax.random.split(key, 32)
    f32 = jnp.float32

    def nrm(k, shape, s):
        return jax.random.normal(k, shape, f32) * s

    def gain(k, shape):
        return 1.0 + 0.02 * jax.random.normal(k, shape, f32)

    D = D_MODEL
    return {
        "x_prompt": nrm(ks[0], (BATCH, SEQ, D), 1.0),
        "x_sample": nrm(ks[1], (DEC_BATCH, DEC_SEQ, D), 1.0),
        "cache_na_k": nrm(ks[2], (DEC_BATCH, DEPTH, PAST_LEN, NA_HEADS, NA_HEAD_DIM), 1.0),
        "cache_na_v": nrm(ks[3], (DEC_BATCH, DEPTH, PAST_LEN, NA_HEADS, NA_HEAD_DIM), 1.0),
        "cache_mla_ckv": nrm(ks[4], (DEC_BATCH, DEPTH, PAST_LEN, MLA_KV_RANK), 1.0),
        "cache_mla_krope": nrm(ks[5], (DEC_BATCH, DEPTH, PAST_LEN, MLA_ROPE_DIM), 1.0),
        "c": nrm(ks[6], (DEC_BATCH, D), 1.0),
        "c_ctx": nrm(ks[7], (D,), 1.0),
        "w_in": nrm(ks[8], (DEPTH, D, IN_COLS), D ** -0.5),
        "w_conv": nrm(ks[9], (DEPTH, CONV_K, CONV_W), CONV_K ** -0.5),
        "rpb": nrm(ks[10], (DEPTH, 2 * NA_WIN_R - 1, 2 * NA_WIN_C - 1, NA_HEADS), 0.1),
        "g_qa": gain(ks[11], (DEPTH, MLA_Q_RANK)),
        "w_qb": nrm(ks[12], (DEPTH, MLA_Q_RANK, MLA_HEADS * (MLA_NOPE_DIM + MLA_ROPE_DIM)), MLA_Q_RANK ** -0.5),
        "g_kva": gain(ks[13], (DEPTH, MLA_KV_RANK)),
        "w_kvb": nrm(ks[14], (DEPTH, MLA_KV_RANK, MLA_HEADS * (MLA_NOPE_DIM + MLA_V_DIM)), MLA_KV_RANK ** -0.5),
        "g_grp": gain(ks[15], (DEPTH, MIX_W)),
        "w_out": nrm(ks[16], (DEPTH, MIX_W, D), MIX_W ** -0.5),
        "w_mod": nrm(ks[17], (DEPTH, D, 6 * D), 0.5 * D ** -0.5),
        "b_mod": nrm(ks[18], (DEPTH, 6 * D), 0.02),
        "g_pre_mix": gain(ks[19], (DEPTH, D)),
        "g_post_mix": gain(ks[20], (DEPTH, D)),
        "g_pre_ff": gain(ks[21], (DEPTH, D)),
        "g_post_ff": gain(ks[22], (DEPTH, D)),
        "w_ff1": nrm(ks[23], (N_DENSE, D, D_FF), D ** -0.5),
        "w_ff3": nrm(ks[24], (N_DENSE, D, D_FF), D ** -0.5),
        "w_ff2": nrm(ks[25], (N_DENSE, D_FF, D), D_FF ** -0.5),
        "w_router": nrm(ks[26], (N_MOE, D, N_EXPERTS), D ** -0.5),
        "b_router": nrm(ks[27], (N_MOE, N_EXPERTS), 0.01),
        "moe_w1": nrm(ks[28], (N_MOE, N_EXPERTS, D, D_FF_EXPERT), D ** -0.5),
        "moe_w3": nrm(ks[29], (N_MOE, N_EXPERTS, D, D_FF_EXPERT), D ** -0.5),
        "moe_w2": nrm(ks[30], (N_MOE, N_EXPERTS, D_FF_EXPERT, D), D_FF_EXPERT ** -0.5),
    }


def reference(x_prompt, x_sample, cache_na_k, cache_na_v, cache_mla_ckv, cache_mla_krope, c, c_ctx,
              w_in, w_conv, rpb, g_qa, w_qb, g_kva, w_kvb, g_grp, w_out, w_mod, b_mod,
              g_pre_mix, g_post_mix, g_pre_ff, g_post_ff, w_ff1, w_ff3, w_ff2,
              w_router, b_router, moe_w1, moe_w3, moe_w2):
    ffn_params = (w_ff1, w_ff3, w_ff2, w_router, b_router, moe_w1, moe_w3, moe_w2)

    y_prompt = x_prompt
    nak, nav, ckvs, krs = [], [], [], []
    for l in range(DEPTH):
        sh1, sc1, gt1, sh2, sc2, gt2 = _adaln(c_ctx[None, :], w_mod[l], b_mod[l])
        h = _rms_norm(y_prompt, g_pre_mix[l]) * (1 + sc1) + sh1
        mix, k_na, v_na, ckv, k_r = _mixer_context(h, w_in[l], w_conv[l], g_qa[l], w_qb[l], g_kva[l], w_kvb[l], g_grp[l], w_out[l])
        y_prompt = y_prompt + gt1 * _rms_norm(mix, g_post_mix[l])
        h = _rms_norm(y_prompt, g_pre_ff[l]) * (1 + sc2) + sh2
        y_prompt = y_prompt + gt2 * _rms_norm(_ffn(h, l, *ffn_params), g_post_ff[l])
        nak.append(k_na)
        nav.append(v_na)
        ckvs.append(ckv)
        krs.append(k_r)
    state_na_k = jnp.stack(nak, axis=1)
    state_na_v = jnp.stack(nav, axis=1)
    state_mla_ckv = jnp.stack(ckvs, axis=1)
    state_mla_krope = jnp.stack(krs, axis=1)

    y_sample = x_sample
    for l in range(DEPTH):
        sh1, sc1, gt1, sh2, sc2, gt2 = _adaln(c, w_mod[l], b_mod[l])
        h = _rms_norm(y_sample, g_pre_mix[l]) * (1 + sc1) + sh1
        mix = _mixer_latent(h, cache_na_k[:, l], cache_na_v[:, l], cache_mla_ckv[:, l], cache_mla_krope[:, l],
                            w_in[l], w_conv[l], rpb[l], g_qa[l], w_qb[l], g_kva[l], w_kvb[l], g_grp[l], w_out[l])
        y_sample = y_sample + gt1 * _rms_norm(mix, g_post_mix[l])
        h = _rms_norm(y_sample, g_pre_ff[l]) * (1 + sc2) + sh2
        y_sample = y_sample + gt2 * _rms_norm(_ffn(h, l, *ffn_params), g_post_ff[l])

    return (y_prompt, y_sample, state_na_k, state_na_v, state_mla_ckv, state_mla_krope)
```

```python
import functools

import numpy as np
import jax
import jax.numpy as jnp
from jax import lax
from jax.experimental import pallas as pl
from jax.experimental.pallas import tpu as pltpu

F32 = jnp.float32
BF16 = jnp.bfloat16

D = 1024
DEPTH = 4
N_CTX_B, CTX_L = 16, 256
N_LAT_B, LAT_L = 2, 4096
GRID_W = 64
GRID_ROWS = LAT_L // GRID_W
GROUP = 4096
T_CTX = N_CTX_B * CTX_L
T_LAT = N_LAT_B * LAT_L
T_ALL = T_CTX + T_LAT
PAST = 256
CONV_W = 256
NA_H, NA_D = 6, 64
NA_W = NA_H * NA_D
WIN_R, WIN_C = 8, 16
MLA_H, NOPE, ROPE, VD = 6, 64, 32, 64
MLA_W = MLA_H * VD
Q_RANK, KV_RANK = 256, 128
IN_MAIN = 3 * CONV_W + 3 * NA_W + Q_RANK + KV_RANK
IN_PAD = IN_MAIN + 256
D_FF = 2816
N_EXP = 8
D_FFE = 3584
EPS = 1e-6
NEG = -1e30
ROPE_BASE = 10000.0
LANE = 128
VMEM_LIMIT = 56 * 1024 * 1024

TM = 512
MOE_TM = 512
MOE_TF = 512
MOE_ROWS = 2 * T_ALL + N_EXP * MOE_TM
MOE_TILES = MOE_ROWS // MOE_TM
GATHER_R = 256


def _rms(x, g):
    return x * lax.rsqrt(jnp.mean(x * x, axis=-1, keepdims=True) + EPS) * g


def _silu(x):
    return x * (1.0 / (1.0 + jnp.exp(-x)))


def _dot(a, b):
    return jnp.dot(a, b, preferred_element_type=F32)


def _dot_nt(a, b):
    return lax.dot_general(a, b, (((1,), (1,)), ((), ())), preferred_element_type=F32)


def _params(sem, vmem=VMEM_LIMIT):
    return pltpu.CompilerParams(dimension_semantics=sem, vmem_limit_bytes=vmem)


def _const_spec(shape):
    nd = len(shape)
    return pl.BlockSpec(shape, lambda *_: (0,) * nd, pipeline_mode=pl.Buffered(1))


MOD_TN = 1536


def _mod_kernel(cond_ref, w_ref, b_ref, o_ref):
    s = _silu(cond_ref[...]).astype(BF16)
    o_ref[0] = _dot(s, w_ref[0].astype(BF16)) + b_ref[0]


def _modulation(cond8, w_mod, b_mod):
    return pl.pallas_call(
        _mod_kernel,
        grid=(DEPTH, 6 * D // MOD_TN),
        in_specs=[
            pl.BlockSpec((8, D), lambda l, n: (0, 0)),
            pl.BlockSpec((1, D, MOD_TN), lambda l, n: (l, 0, n)),
            pl.BlockSpec((1, 1, MOD_TN), lambda l, n: (l, 0, n)),
        ],
        out_specs=pl.BlockSpec((1, 8, MOD_TN), lambda l, n: (l, 0, n)),
        out_shape=jax.ShapeDtypeStruct((DEPTH, 8, 6 * D), F32),
        compiler_params=_params(("arbitrary", "arbitrary")),
        name="adaln_mod",
    )(cond8, w_mod, b_mod.reshape(DEPTH, 1, 6 * D))


def _inproj_kernel(y_ref, mod_ref, gpre_ref, w_ref, gqa_ref, gkva_ref,
                   bu_ref, q_ref, k_ref, v_ref, qa_ref, ckv_ref, kr_ref):
    mod = mod_ref[0]
    h = _rms(y_ref[...], gpre_ref[...]) * (1.0 + mod[1:2]) + mod[0:1]
    z = _dot(h.astype(BF16), w_ref[...])
    bu_ref[:, 0:CONV_W] = z[:, 0:CONV_W]
    bu_ref[:, CONV_W:2 * CONV_W] = z[:, CONV_W:2 * CONV_W] * z[:, 2 * CONV_W:3 * CONV_W]
    o = 3 * CONV_W
    q_ref[...] = (z[:, o:o + NA_W] * (NA_D ** -0.5)).astype(BF16)
    k_ref[...] = z[:, o + NA_W:o + 2 * NA_W]
    v_ref[...] = z[:, o + 2 * NA_W:o + 3 * NA_W]
    o += 3 * NA_W
    qa_ref[...] = _rms(z[:, o:o + Q_RANK], gqa_ref[...]).astype(BF16)
    o += Q_RANK
    ckv_ref[...] = _rms(z[:, o:o + KV_RANK], gkva_ref[...])
    kr_ref[...] = z[:, IN_MAIN:IN_PAD]


def _in_projection(y, mod_l, g_pre, w_in_p, g_qa, g_kva):
    rows = lambda w: pl.BlockSpec((TM, w), lambda i: (i, 0))
    shp = lambda w, dt: jax.ShapeDtypeStruct((T_ALL, w), dt)
    return pl.pallas_call(
        _inproj_kernel,
        grid=(T_ALL // TM,),
        in_specs=[
            rows(D),
            pl.BlockSpec((1, 6, D), lambda i: (i * TM // GROUP, 0, 0)),
            _const_spec((1, D)),
            _const_spec((D, IN_PAD)),
            _const_spec((1, Q_RANK)),
            _const_spec((1, KV_RANK)),
        ],
        out_specs=[rows(2 * CONV_W), rows(NA_W), rows(NA_W), rows(NA_W), rows(Q_RANK), rows(KV_RANK), rows(2 * LANE)],
        out_shape=[shp(2 * CONV_W, F32), shp(NA_W, BF16), shp(NA_W, F32), shp(NA_W, F32),
                   shp(Q_RANK, BF16), shp(KV_RANK, F32), shp(2 * LANE, F32)],
        compiler_params=_params(("arbitrary",)),
        name="prenorm_inproj",
    )(y, mod_l, g_pre, w_in_p, g_qa, g_kva)


def _lane_lo():
    return lax.broadcasted_iota(jnp.int32, (1, LANE), 1) < NA_D


def _pair_attention(q2, keys, vals, biases):
    lo = _lane_lo()
    outs = []
    for u in range(2):
        qm = jnp.where(lo if u == 0 else jnp.logical_not(lo), q2, jnp.zeros_like(q2))
        ss = []
        for kk, bb in zip(keys, biases):
            s = _dot_nt(qm, kk)
            if bb is not None:
                s = s + bb(u)
            ss.append(s)
        m = ss[0].max(axis=-1, keepdims=True)
        for s in ss[1:]:
            m = jnp.maximum(m, s.max(axis=-1, keepdims=True))
        acc = None
        l = None
        for s, vv in zip(ss, vals):
            p = jnp.exp(s - m)
            ls = p.sum(axis=-1, keepdims=True)
            o = _dot(p.astype(BF16), vv)
            acc = o if acc is None else acc + o
            l = ls if l is None else l + ls
        outs.append(acc * (1.0 / l))
    return jnp.where(lo, outs[0], outs[1])


def _ctx_attn_kernel(q_ref, k_ref, v_ref, qa_ref, ckv_ref, kr_ref, wq_ref, wk_ref, wv_ref, o_ref):
    for j in range(NA_H // 2):
        sl = slice(LANE * j, LANE * (j + 1))
        k2 = k_ref[:, sl].astype(BF16)
        v2 = v_ref[:, sl].astype(BF16)
        o_ref[:, sl] = _pair_attention(q_ref[:, sl], [k2], [v2], [None]).astype(BF16)
    qa = qa_ref[...]
    ckv = ckv_ref[...].astype(BF16)
    kr = kr_ref[:, 0:LANE]
    scale = (NOPE + ROPE) ** -0.5
    lo = _lane_lo()
    for j in range(MLA_H // 2):
        v2 = _dot(ckv, wv_ref[j]).astype(BF16)
        outs = []
        for u in range(2):
            h = 2 * j + u
            qh = (_dot(qa, wq_ref[h]) * scale).astype(BF16)
            kh = (_dot(ckv, wk_ref[h]) + kr).astype(BF16)
            s = _dot_nt(qh, kh)
            m = s.max(axis=-1, keepdims=True)
            p = jnp.exp(s - m)
            l = p.sum(axis=-1, keepdims=True)
            outs.append(_dot(p.astype(BF16), v2) * (1.0 / l))
        o_ref[:, NA_W + LANE * j:NA_W + LANE * (j + 1)] = jnp.where(lo, outs[0], outs[1]).astype(BF16)


def _ctx_attention(q, k, v, qa, ckv, kr, wq, wk, wv2):
    rows = lambda w: pl.BlockSpec((CTX_L, w), lambda b: (b, 0))
    return pl.pallas_call(
        _ctx_attn_kernel,
        grid=(N_CTX_B,),
        in_specs=[rows(NA_W), rows(NA_W), rows(NA_W), rows(Q_RANK), rows(KV_RANK), rows(2 * LANE),
                  _const_spec((MLA_H, Q_RANK, LANE)), _const_spec((MLA_H, KV_RANK, LANE)),
                  _const_spec((MLA_H // 2, KV_RANK, LANE))],
        out_specs=rows(NA_W + MLA_W),
        out_shape=jax.ShapeDtypeStruct((T_CTX, NA_W + MLA_W), BF16),
        compiler_params=_params(("arbitrary",)),
        name="ctx_attention",
    )(q, k, v, qa, ckv, kr, wq, wk, wv2)


NA_QROWS = 8
NA_KROWS = 16
NA_TQ = NA_QROWS * GRID_W
NA_TK = NA_KROWS * GRID_W
NA_RB = GRID_ROWS // NA_QROWS
NA_VARIANTS = ((0, 0), (NA_QROWS, NA_QROWS - WIN_R // 2), (GRID_ROWS - NA_QROWS, GRID_ROWS - NA_KROWS))


def _na_bias_kernel(t2_ref, o_ref):
    var = pl.program_id(0)
    lo = _lane_lo()
    neg = jnp.full((GRID_W, LANE), NEG, F32)
    for vi, (r0, s0) in enumerate(NA_VARIANTS):
        @pl.when(var == vi)
        def _(r0=r0, s0=s0):
            for qi in range(NA_QROWS):
                qr = r0 + qi
                rs = min(max(qr - WIN_R // 2, 0), GRID_ROWS - WIN_R)
                for m in range(NA_KROWS // 2):
                    kl = s0 + 2 * m
                    ok_l = rs <= kl < rs + WIN_R
                    ok_r = rs <= kl + 1 < rs + WIN_R
                    dl = kl - qr
                    if (ok_l or ok_r) and -WIN_R <= dl <= WIN_R - 1:
                        blk = t2_ref[0, dl + WIN_R]
                        if not ok_l:
                            blk = jnp.where(lo, NEG, blk)
                        if not ok_r:
                            blk = jnp.where(lo, blk, NEG)
                    else:
                        blk = neg
                    o_ref[0, 0, GRID_W * qi:GRID_W * (qi + 1), LANE * m:LANE * (m + 1)] = blk


def _na_bias(t2):
    return pl.pallas_call(
        _na_bias_kernel,
        grid=(3, NA_H),
        in_specs=[pl.BlockSpec((1, 2 * WIN_R, GRID_W, LANE), lambda v, h: (h, 0, 0, 0))],
        out_specs=pl.BlockSpec((1, 1, NA_TQ, NA_TK), lambda v, h: (v, h, 0, 0)),
        out_shape=jax.ShapeDtypeStruct((3, NA_H, NA_TQ, NA_TK), F32),
        compiler_params=_params(("arbitrary", "arbitrary")),
        name="na_bias_tiles",
    )(t2)


def _na_lat_kernel(q_ref, k_ref, v_ref, kc_ref, vc_ref, b_ref, o_ref):
    rb = pl.program_id(2)
    start = pl.multiple_of(jnp.clip(NA_QROWS * rb - WIN_R // 2, 0, GRID_ROWS - NA_KROWS) * GRID_W, 256)
    ks = k_ref[0, pl.ds(start, NA_TK), :].astype(BF16)
    vs = v_ref[0, pl.ds(start, NA_TK), :].astype(BF16)
    kc = kc_ref[0].astype(BF16)
    vc = vc_ref[0].astype(BF16)
    o = _pair_attention(q_ref[...], [ks, kc], [vs, vc], [lambda u: b_ref[0, u], None])
    o_ref[...] = o.astype(BF16)


def _na_latent(q, k3, v3, kc, vc, bias):
    def variant(rb):
        return jnp.where(rb == 0, 0, jnp.where(rb == NA_RB - 1, 2, 1))

    qoff = T_CTX // NA_TQ
    return pl.pallas_call(
        _na_lat_kernel,
        grid=(N_LAT_B, NA_H // 2, NA_RB),
        in_specs=[
            pl.BlockSpec((NA_TQ, LANE), lambda b, j, r: (qoff + b * NA_RB + r, j)),
            pl.BlockSpec((1, LAT_L, LANE), lambda b, j, r: (1 + b, 0, j)),
            pl.BlockSpec((1, LAT_L, LANE), lambda b, j, r: (1 + b, 0, j)),
            pl.BlockSpec((1, PAST, LANE), lambda b, j, r: (b, 0, j)),
            pl.BlockSpec((1, PAST, LANE), lambda b, j, r: (b, 0, j)),
            pl.BlockSpec((1, 2, NA_TQ, NA_TK), lambda b, j, r: (variant(r), j, 0, 0)),
        ],
        out_specs=pl.BlockSpec((NA_TQ, LANE), lambda b, j, r: (b * NA_RB + r, j)),
        out_shape=jax.ShapeDtypeStruct((T_LAT, NA_W), BF16),
        compiler_params=_params(("arbitrary", "arbitrary", "arbitrary")),
        name="na_latent",
    )(q, k3, v3, kc, vc, bias)


MLA_TP = 512


def _mla_prep_kernel(*refs, with_q):
    if with_q:
        (qa_ref, ckv_ref, kr_ref, cq_ref, sq_ref, ck_ref, sk_ref,
         wq_ref, wqr_ref, wk_ref, wv_ref, qo_ref, ko_ref, vo_ref) = refs
    else:
        ckv_ref, kr_ref, ck_ref, sk_ref, wk_ref, wv_ref, ko_ref, vo_ref = refs
    ckv = ckv_ref[...].astype(BF16)
    krope = kr_ref[:, 0:LANE] * ck_ref[...] + kr_ref[:, LANE:2 * LANE] * sk_ref[...]
    for h in range(MLA_H):
        ko_ref[h] = (_dot(ckv, wk_ref[h]) + krope).astype(BF16)
    for j in range(MLA_H // 2):
        vo_ref[j] = _dot(ckv, wv_ref[j]).astype(BF16)
    if with_q:
        qa = qa_ref[...]
        scale = (NOPE + ROPE) ** -0.5
        for h in range(MLA_H):
            qh = _dot(qa, wq_ref[h]) * cq_ref[...] + _dot(qa, wqr_ref[h]) * sq_ref[...]
            qo_ref[h] = (qh * scale).astype(BF16)


def _mla_prep_latent(qa, ckv, kr, cosq, sinq, cosk, sink, wq, wqr, wk, wv2):
    off = T_CTX // MLA_TP
    nblk = LAT_L // MLA_TP
    rows = lambda w: pl.BlockSpec((MLA_TP, w), lambda b, i: (off + b * nblk + i, 0))
    tab = pl.BlockSpec((MLA_TP, LANE), lambda b, i: (i, 0))
    outh = lambda n: pl.BlockSpec((None, n, MLA_TP, LANE), lambda b, i: (b, 0, i, 0))
    return pl.pallas_call(
        functools.partial(_mla_prep_kernel, with_q=True),
        grid=(N_LAT_B, nblk),
        in_specs=[rows(Q_RANK), rows(KV_RANK), rows(2 * LANE), tab, tab, tab, tab,
                  _const_spec((MLA_H, Q_RANK, LANE)), _const_spec((MLA_H, Q_RANK, LANE)),
                  _const_spec((MLA_H, KV_RANK, LANE)), _const_spec((MLA_H // 2, KV_RANK, LANE))],
        out_specs=[outh(MLA_H), outh(MLA_H), outh(MLA_H // 2)],
        out_shape=[jax.ShapeDtypeStruct((N_LAT_B, MLA_H, LAT_L, LANE), BF16),
                   jax.ShapeDtypeStruct((N_LAT_B, MLA_H, LAT_L, LANE), BF16),
                   jax.ShapeDtypeStruct((N_LAT_B, MLA_H // 2, LAT_L, LANE), BF16)],
        compiler_params=_params(("arbitrary", "arbitrary")),
        name="mla_prep_latent",
    )(qa, ckv, kr, cosq, sinq, cosk, sink, wq, wqr, wk, wv2)


def _mla_prep_cache(ckv, kr, ones, zeros, wk, wv2):
    rows = lambda w: pl.BlockSpec((PAST, w), lambda b: (b, 0))
    tab = pl.BlockSpec((PAST, LANE), lambda b: (0, 0))
    outh = lambda n: pl.BlockSpec((None, n, PAST, LANE), lambda b: (b, 0, 0, 0))
    return pl.pallas_call(
        functools.partial(_mla_prep_kernel, with_q=False),
        grid=(N_LAT_B,),
        in_specs=[rows(KV_RANK), rows(2 * LANE), tab, tab,
                  _const_spec((MLA_H, KV_RANK, LANE)), _const_spec((MLA_H // 2, KV_RANK, LANE))],
        out_specs=[outh(MLA_H), outh(MLA_H // 2)],
        out_shape=[jax.ShapeDtypeStruct((N_LAT_B, MLA_H, PAST, LANE), BF16),
                   jax.ShapeDtypeStruct((N_LAT_B, MLA_H // 2, PAST, LANE), BF16)],
        compiler_params=_params(("arbitrary",)),
        name="mla_prep_cache",
    )(ckv, kr, ones, zeros, wk, wv2)


MLA_TQ = 256


def _mla_attn_kernel(q_ref, k_ref, kc_ref, v_ref, vc_ref, o_ref):
    lo = _lane_lo()
    v2 = v_ref[0]
    vc2 = vc_ref[0]
    outs = []
    for u in range(2):
        qh = q_ref[u]
        s = _dot_nt(qh, k_ref[u])
        sc = _dot_nt(qh, kc_ref[u])
        m = jnp.maximum(s.max(axis=-1, keepdims=True), sc.max(axis=-1, keepdims=True))
        p = jnp.exp(s - m)
        pc = jnp.exp(sc - m)
        l = p.sum(axis=-1, keepdims=True) + pc.sum(axis=-1, keepdims=True)
        o = _dot(p.astype(BF16), v2) + _dot(pc.astype(BF16), vc2)
        outs.append(o * (1.0 / l))
    o_ref[...] = jnp.where(lo, outs[0], outs[1]).astype(BF16)


def _mla_attention(qh, kh, kch, vh, vch):
    nq = LAT_L // MLA_TQ
    return pl.pallas_call(
        _mla_attn_kernel,
        grid=(N_LAT_B, MLA_H // 2, nq),
        in_specs=[
            pl.BlockSpec((None, 2, MLA_TQ, LANE), lambda b, j, i: (b, j, i, 0)),
            pl.BlockSpec((None, 2, LAT_L, LANE), lambda b, j, i: (b, j, 0, 0)),
            pl.BlockSpec((None, 2, PAST, LANE), lambda b, j, i: (b, j, 0, 0)),
            pl.BlockSpec((None, 1, LAT_L, LANE), lambda b, j, i: (b, j, 0, 0)),
            pl.BlockSpec((None, 1, PAST, LANE), lambda b, j, i: (b, j, 0, 0)),
        ],
        out_specs=pl.BlockSpec((MLA_TQ, LANE), lambda b, j, i: (b * nq + i, j)),
        out_shape=jax.ShapeDtypeStruct((T_LAT, MLA_W), BF16),
        compiler_params=_params(("arbitrary", "arbitrary", "arbitrary")),
        name="mla_attention",
    )(qh, kh, kch, vh, vch)


def _merge_kernel(*refs, moe):
    if moe:
        (bu_ref, up_ref, un_ref, att_ref, y_ref, mod_ref, wconv_ref, ggrp_ref, wout_ref, gpost_ref, gpre_ref,
         wr_hi_ref, wr_lo_ref, br_ref, yo_ref, h_ref, route_ref) = refs
    else:
        (bu_ref, up_ref, un_ref, att_ref, y_ref, mod_ref, wconv_ref, ggrp_ref, wout_ref, gpost_ref, gpre_ref,
         yo_ref, h_ref) = refs
    i = pl.program_id(0)
    row = i * TM + lax.broadcasted_iota(jnp.int32, (TM, 1), 0)
    seq_mask = jnp.where(row < T_CTX, CTX_L - 1, LAT_L - 1)
    pos = jnp.bitwise_and(row, seq_mask)
    local = lax.broadcasted_iota(jnp.int32, (TM, 1), 0)
    u = bu_ref[:, CONV_W:2 * CONV_W]
    prev = jnp.where(local == 0, up_ref[7:8, :], pltpu.roll(u, 1, axis=0))
    prev = jnp.where(pos == 0, 0.0, prev)
    nxt = jnp.where(local == TM - 1, un_ref[0:1, :], pltpu.roll(u, TM - 1, axis=0))
    nxt = jnp.where(pos == seq_mask, 0.0, nxt)
    wc = wconv_ref[...]
    conv = bu_ref[:, 0:CONV_W] * (wc[0:1] * prev + wc[1:2] * u + wc[2:3] * nxt)
    gg = ggrp_ref[...]
    att = att_ref[...].astype(F32)
    mixin = jnp.concatenate([
        _rms(conv, gg[:, 0:CONV_W]).astype(BF16),
        _rms(att[:, 0:NA_W], gg[:, CONV_W:CONV_W + NA_W]).astype(BF16),
        _rms(att[:, NA_W:], gg[:, CONV_W + NA_W:]).astype(BF16)], axis=-1)
    mix = _dot(mixin, wout_ref[...])
    mod = mod_ref[0]
    y = y_ref[...] + mod[2:3] * _rms(mix, gpost_ref[...])
    yo_ref[...] = y
    h = _rms(y, gpre_ref[...]) * (1.0 + mod[4:5]) + mod[3:4]
    if not moe:
        h_ref[...] = h.astype(BF16)
        return
    h_ref[...] = h
    hi = h.astype(BF16)
    lo = (h - hi.astype(F32)).astype(BF16)
    logits = _dot(hi, wr_hi_ref[...]) + (_dot(lo, wr_hi_ref[...]) + _dot(hi, wr_lo_ref[...])) + br_ref[...]
    lane = lax.broadcasted_iota(jnp.int32, (TM, LANE), 1).astype(F32)
    logits = jnp.where(lane < N_EXP, logits, NEG)
    m1 = logits.max(axis=-1, keepdims=True)
    i1 = jnp.where(logits == m1, lane, float(LANE)).min(axis=-1, keepdims=True)
    rest = jnp.where(lane == i1, NEG, logits)
    m2 = rest.max(axis=-1, keepdims=True)
    i2 = jnp.where(rest == m2, lane, float(LANE)).min(axis=-1, keepdims=True)
    e2 = jnp.exp(m2 - m1)
    g1 = 1.0 / (1.0 + e2)
    g2 = e2 * g1
    route_ref[...] = jnp.where(lane == 0, i1, jnp.where(lane == 1, i2, jnp.where(lane == 2, g1, jnp.where(lane == 3, g2, 0.0))))


def _merge(bu, att, y, mod_l, w_conv, g_grp, w_out, g_post, g_pre_ff, router=None):
    moe = router is not None
    nb8 = TM // 8
    last8 = T_ALL // 8 - 1
    rows = lambda w: pl.BlockSpec((TM, w), lambda i: (i, 0))
    in_specs = [
        rows(2 * CONV_W),
        pl.BlockSpec((8, CONV_W), lambda i: (jnp.maximum(i * nb8 - 1, 0), 1)),
        pl.BlockSpec((8, CONV_W), lambda i: (jnp.minimum((i + 1) * nb8, last8), 1)),
        rows(NA_W + MLA_W),
        rows(D),
        pl.BlockSpec((1, 6, D), lambda i: (i * TM // GROUP, 0, 0)),
        _const_spec((3, CONV_W)),
        _const_spec((1, D)),
        _const_spec((D, D)),
        _const_spec((1, D)),
        _const_spec((1, D)),
    ]
    args = [bu, bu, bu, att, y, mod_l, w_conv, g_grp, w_out, g_post, g_pre_ff]
    out_specs = [rows(D), rows(D)]
    out_shape = [jax.ShapeDtypeStruct((T_ALL, D), F32), jax.ShapeDtypeStruct((T_ALL, D), F32 if moe else BF16)]
    if moe:
        in_specs += [_const_spec((D, LANE)), _const_spec((D, LANE)), _const_spec((1, LANE))]
        args += list(router)
        out_specs.append(rows(LANE))
        out_shape.append(jax.ShapeDtypeStruct((T_ALL, LANE), F32))
    return pl.pallas_call(
        functools.partial(_merge_kernel, moe=moe),
        grid=(T_ALL // TM,),
        in_specs=in_specs,
        out_specs=out_specs,
        out_shape=out_shape,
        compiler_params=_params(("arbitrary",)),
        name="merge_outproj_moe" if moe else "merge_outproj",
    )(*args)


FF_CHUNK = D_FF // 2


def _ffn_kernel(h_ref, y_ref, mod_ref, w1_ref, w3_ref, w2_ref, gpost_ref, o_ref):
    h = h_ref[...]
    acc = None
    for c in range(D_FF // FF_CHUNK):
        sl = slice(c * FF_CHUNK, (c + 1) * FF_CHUNK)
        a1 = _dot(h, w1_ref[:, sl])
        a3 = _dot(h, w3_ref[:, sl])
        o = _dot((_silu(a1) * a3).astype(BF16), w2_ref[sl, :])
        acc = o if acc is None else acc + o
    o_ref[...] = y_ref[...] + mod_ref[0][5:6] * _rms(acc, gpost_ref[...])


def _dense_ffn(h, y, mod_l, w1, w3, w2, g_post):
    rows = lambda w: pl.BlockSpec((TM, w), lambda i: (i, 0))
    return pl.pallas_call(
        _ffn_kernel,
        grid=(T_ALL // TM,),
        in_specs=[rows(D), rows(D), pl.BlockSpec((1, 6, D), lambda i: (i * TM // GROUP, 0, 0)),
                  _const_spec((D, D_FF)), _const_spec((D, D_FF)), _const_spec((D_FF, D)), _const_spec((1, D))],
        out_specs=rows(D),
        out_shape=jax.ShapeDtypeStruct((T_ALL, D), F32),
        compiler_params=_params(("arbitrary",)),
        name="dense_ffn",
    )(h, y, mod_l, w1, w3, w2, g_post)


def _gather_kernel(idx_ref, src_ref, o_ref, sem):
    base = pl.program_id(0) * GATHER_R

    def row_copy(r):
        return pltpu.make_async_copy(src_ref.at[pl.ds(idx_ref[base + r], 1), :], o_ref.at[pl.ds(r, 1), :], sem)

    def issue(r, c):
        row_copy(r).start()
        return c

    def drain(r, c):
        row_copy(r).wait()
        return c

    lax.fori_loop(0, GATHER_R, issue, 0)
    lax.fori_loop(0, GATHER_R, drain, 0)


def _gather_rows(idx, src):
    n = idx.shape[0]
    return pl.pallas_call(
        _gather_kernel,
        grid_spec=pltpu.PrefetchScalarGridSpec(
            num_scalar_prefetch=1,
            grid=(n // GATHER_R,),
            in_specs=[pl.BlockSpec(memory_space=pl.ANY)],
            out_specs=pl.BlockSpec((GATHER_R, D), lambda i, idx: (i, 0)),
            scratch_shapes=[pltpu.SemaphoreType.DMA(())],
        ),
        out_shape=jax.ShapeDtypeStruct((n, D), F32),
        compiler_params=_params(("arbitrary",)),
        name="gather_rows",
    )(idx, src)


def _moe_kernel(te_ref, nused_ref, x_ref, w1_ref, w3_ref, w2_ref, o_ref, acc_ref):
    i = pl.program_id(0)
    j = pl.program_id(1)
    nj = pl.num_programs(1)
    used = i < nused_ref[0]

    @pl.when(jnp.logical_and(used, j == 0))
    def _():
        acc_ref[...] = jnp.zeros_like(acc_ref)

    @pl.when(used)
    def _():
        x = x_ref[...].astype(BF16)
        a1 = _dot(x, w1_ref[0].astype(BF16))
        a3 = _dot(x, w3_ref[0].astype(BF16))
        acc_ref[...] += _dot((_silu(a1) * a3).astype(BF16), w2_ref[0].astype(BF16))

    @pl.when(jnp.logical_and(used, j == nj - 1))
    def _():
        o_ref[...] = acc_ref[...]

    @pl.when(jnp.logical_and(jnp.logical_not(used), j == nj - 1))
    def _():
        o_ref[...] = jnp.zeros_like(o_ref)


def _moe_experts(tile_expert, n_used, xs, w1, w3, w2):
    nj = D_FFE // MOE_TF

    def jj(i, j, nu):
        return jnp.where(i < nu[0], j, nj - 1)

    return pl.pallas_call(
        _moe_kernel,
        grid_spec=pltpu.PrefetchScalarGridSpec(
            num_scalar_prefetch=2,
            grid=(MOE_TILES, nj),
            in_specs=[
                pl.BlockSpec((MOE_TM, D), lambda i, j, te, nu: (i, 0)),
                pl.BlockSpec((1, D, MOE_TF), lambda i, j, te, nu: (te[i], 0, jj(i, j, nu))),
                pl.BlockSpec((1, D, MOE_TF), lambda i, j, te, nu: (te[i], 0, jj(i, j, nu))),
                pl.BlockSpec((1, MOE_TF, D), lambda i, j, te, nu: (te[i], jj(i, j, nu), 0)),
            ],
            out_specs=pl.BlockSpec((MOE_TM, D), lambda i, j, te, nu: (i, 0)),
            scratch_shapes=[pltpu.VMEM((MOE_TM, D), F32)],
        ),
        out_shape=jax.ShapeDtypeStruct((MOE_ROWS, D), F32),
        compiler_params=_params(("arbitrary", "arbitrary")),
        name="moe_experts",
    )(tile_expert, n_used, xs, w1, w3, w2)


def _moe_combine_kernel(z_ref, route_ref, y_ref, mod_ref, gpost_ref, o_ref):
    r = route_ref[...]
    f = r[:, 2:3] * z_ref[0] + r[:, 3:4] * z_ref[1]
    o_ref[...] = y_ref[...] + mod_ref[0][5:6] * _rms(f, gpost_ref[...])


def _moe_combine(z, route, y, mod_l, g_post):
    rows = lambda w: pl.BlockSpec((TM, w), lambda i: (i, 0))
    return pl.pallas_call(
        _moe_combine_kernel,
        grid=(T_ALL // TM,),
        in_specs=[pl.BlockSpec((2, TM, D), lambda i: (0, i, 0)), rows(LANE), rows(D),
                  pl.BlockSpec((1, 6, D), lambda i: (i * TM // GROUP, 0, 0)), _const_spec((1, D))],
        out_specs=rows(D),
        out_shape=jax.ShapeDtypeStruct((T_ALL, D), F32),
        compiler_params=_params(("arbitrary",)),
        name="moe_combine",
    )(z, route, y, mod_l, g_post)


def _moe_plan(route):
    e = route[:, 0:2].astype(jnp.int32)
    onehot = (e.reshape(-1)[:, None] == jnp.arange(N_EXP, dtype=jnp.int32)[None, :]).astype(jnp.int32)
    csum = jnp.cumsum(onehot, axis=0)
    counts = csum[-1]
    rank = jnp.sum((csum - 1) * onehot, axis=1)
    padded = (counts + MOE_TM - 1) // MOE_TM * MOE_TM
    ends = jnp.cumsum(padded)
    starts = ends - padded
    pos = jnp.sum(starts[None, :] * onehot, axis=1) + rank
    token = jnp.arange(2 * T_ALL, dtype=jnp.int32) // 2
    src = jnp.zeros((MOE_ROWS,), jnp.int32).at[pos].set(token)
    tile_start = jnp.arange(MOE_TILES, dtype=jnp.int32) * MOE_TM
    n_used = (ends[-1] // MOE_TM).astype(jnp.int32)
    tile_expert = jnp.sum((tile_start[:, None] >= ends[None, :]).astype(jnp.int32), axis=1)
    last = jnp.minimum(tile_expert[jnp.maximum(n_used - 1, 0)], N_EXP - 1)
    tile_expert = jnp.where(jnp.arange(MOE_TILES) < n_used, jnp.minimum(tile_expert, N_EXP - 1), last).astype(jnp.int32)
    pos2 = pos.reshape(T_ALL, 2).T.reshape(-1)
    return src, pos2.astype(jnp.int32), tile_expert, n_used.reshape(1)


def _rope_tables():
    half, quarter = ROPE // 2, ROPE // 4
    t = np.arange(LAT_L)
    pos = np.stack([t // GRID_W, t % GRID_W], axis=-1).astype(np.float32)
    inv_freq = np.power(np.float32(ROPE_BASE), -np.arange(quarter, dtype=np.float32) * np.float32(2.0) / np.float32(half))
    ang = pos[:, :, None] * inv_freq.astype(np.float32)
    ang = np.concatenate([ang, ang], axis=-1).reshape(LAT_L, ROPE).astype(np.float32)
    cos, sin = np.cos(ang).astype(np.float32), np.sin(ang).astype(np.float32)
    z64, z32 = np.zeros((LAT_L, NOPE), np.float32), np.zeros((LAT_L, LANE - NOPE - ROPE), np.float32)
    cosq = np.concatenate([np.ones((LAT_L, NOPE), np.float32), cos, z32], axis=1)
    cosk = np.concatenate([z64, cos, z32], axis=1)
    sin_t = np.concatenate([z64, sin, z32], axis=1)
    return jnp.asarray(cosq), jnp.asarray(sin_t), jnp.asarray(cosk), jnp.asarray(sin_t)


def _rot_cols(w):
    k = w.shape[:-1]
    w4 = w.reshape(k + (2, 2, ROPE // 4))
    return jnp.concatenate([-w4[..., 1:2, :], w4[..., 0:1, :]], axis=-2).reshape(k + (ROPE,))


def _prep_weights(w_in, w_qb, w_kvb, w_router, rpb):
    zeros = lambda *s: jnp.zeros(s, F32)
    w_kr = w_in[:, :, IN_MAIN:]
    w_in_p = jnp.concatenate([
        w_in[:, :, :IN_MAIN],
        zeros(DEPTH, D, NOPE), w_kr, zeros(DEPTH, D, LANE - NOPE - ROPE),
        zeros(DEPTH, D, NOPE), _rot_cols(w_kr), zeros(DEPTH, D, LANE - NOPE - ROPE)], axis=-1).astype(BF16)
    wq4 = w_qb.reshape(DEPTH, Q_RANK, MLA_H, NOPE + ROPE).transpose(0, 2, 1, 3)
    pad_q = zeros(DEPTH, MLA_H, Q_RANK, LANE - NOPE - ROPE)
    wq = jnp.concatenate([wq4, pad_q], axis=-1).astype(BF16)
    wqr = jnp.concatenate([zeros(DEPTH, MLA_H, Q_RANK, NOPE), _rot_cols(wq4[..., NOPE:]), pad_q], axis=-1).astype(BF16)
    wkv4 = w_kvb.reshape(DEPTH, KV_RANK, MLA_H, NOPE + VD).transpose(0, 2, 1, 3)
    wk = jnp.concatenate([wkv4[..., :NOPE], zeros(DEPTH, MLA_H, KV_RANK, LANE - NOPE)], axis=-1).astype(BF16)
    wv = wkv4[..., NOPE:]
    wv2 = jnp.concatenate([wv[:, 0::2], wv[:, 1::2]], axis=-1).astype(BF16)
    wr = jnp.pad(w_router, ((0, 0), (0, 0), (0, LANE - N_EXP)))
    wr_hi = wr.astype(BF16)
    wr_lo = (wr - wr_hi.astype(F32)).astype(BF16)
    cpos = np.arange(GRID_W)
    col_start = np.clip(cpos - WIN_C // 2, 0, GRID_W - WIN_C)
    col_in = (cpos[None, :] >= col_start[:, None]) & (cpos[None, :] < col_start[:, None] + WIN_C)
    ci = np.clip(cpos[None, :] - cpos[:, None], -(WIN_C - 1), WIN_C - 1) + (WIN_C - 1)
    tab = rpb[:, :, ci, :]
    tab = jnp.where(col_in[None, None, :, :, None], tab, NEG).transpose(0, 4, 1, 2, 3)
    negs = jnp.full((DEPTH, NA_H, 1, GRID_W, GRID_W), NEG, F32)
    text = jnp.concatenate([negs, tab, negs], axis=2)
    t2 = jnp.concatenate([text[:, :, :-1], text[:, :, 1:]], axis=-1)
    return w_in_p, wq, wqr, wk, wv2, wr_hi, wr_lo, t2


def kernel(x_prompt, x_sample, cache_na_k, cache_na_v, cache_mla_ckv, cache_mla_krope, c, c_ctx,
           w_in, w_conv, rpb, g_qa, w_qb, g_kva, w_kvb, g_grp, w_out, w_mod, b_mod,
           g_pre_mix, g_post_mix, g_pre_ff, g_post_ff, w_ff1, w_ff3, w_ff2,
           w_router, b_router, moe_w1, moe_w3, moe_w2):
    y = jnp.concatenate([x_prompt.reshape(T_CTX, D), x_sample.reshape(T_LAT, D)], axis=0)
    cond8 = jnp.concatenate([c_ctx[None, :], c, jnp.zeros((5, D), F32)], axis=0)
    mods = _modulation(cond8, w_mod, b_mod)[:, :3].reshape(DEPTH, 3, 6, D)

    w_in_p, wq, wqr, wk, wv2, wr_hi, wr_lo, t2 = _prep_weights(w_in, w_qb, w_kvb, w_router, rpb)
    w_out_b = w_out.astype(BF16)
    ff1, ff3, ff2 = w_ff1.astype(BF16), w_ff3.astype(BF16), w_ff2.astype(BF16)
    cosq, sinq, cosk, sink = _rope_tables()
    ones_t = jnp.ones((PAST, LANE), F32)
    zeros_t = jnp.zeros((PAST, LANE), F32)
    row1 = lambda a, l: a[l][None, :]

    st_k, st_v, st_ckv, st_kr = [], [], [], []
    for l in range(DEPTH):
        mod_l = mods[l]
        bu, q, k, v, qa, ckv, kr = _in_projection(y, mod_l, row1(g_pre_mix, l), w_in_p[l], row1(g_qa, l), row1(g_kva, l))
        st_k.append(k[:T_CTX])
        st_v.append(v[:T_CTX])
        st_ckv.append(ckv[:T_CTX])
        st_kr.append(kr[:T_CTX, NOPE:NOPE + ROPE])

        att_ctx = _ctx_attention(q, k, v, qa, ckv, kr, wq[l], wk[l], wv2[l])

        bias = _na_bias(t2[l])
        kc = cache_na_k[:, l].reshape(N_LAT_B, PAST, NA_W)
        vc = cache_na_v[:, l].reshape(N_LAT_B, PAST, NA_W)
        na_lat = _na_latent(q, k.reshape(3, GROUP, NA_W), v.reshape(3, GROUP, NA_W), kc, vc, bias)

        qh, kh, vh = _mla_prep_latent(qa, ckv, kr, cosq, sinq, cosk, sink, wq[l], wqr[l], wk[l], wv2[l])
        c_kr = jnp.pad(cache_mla_krope[:, l].reshape(N_LAT_B * PAST, ROPE), ((0, 0), (NOPE, 2 * LANE - NOPE - ROPE)))
        kch, vch = _mla_prep_cache(cache_mla_ckv[:, l].reshape(N_LAT_B * PAST, KV_RANK), c_kr, ones_t, zeros_t, wk[l], wv2[l])
        mla_lat = _mla_attention(qh, kh, kch, vh, vch)

        att = jnp.concatenate([att_ctx, jnp.concatenate([na_lat, mla_lat], axis=1)], axis=0)
        i = l // 2
        if l % 2 == 0:
            y, h = _merge(bu, att, y, mod_l, w_conv[l], row1(g_grp, l), w_out_b[l], row1(g_post_mix, l), row1(g_pre_ff, l))
            y = _dense_ffn(h, y, mod_l, ff1[i], ff3[i], ff2[i], row1(g_post_ff, l))
        else:
            b_r = jnp.pad(b_router[i], (0, LANE - N_EXP))[None, :]
            y, h, route = _merge(bu, att, y, mod_l, w_conv[l], row1(g_grp, l), w_out_b[l], row1(g_post_mix, l),
                                 row1(g_pre_ff, l), router=(wr_hi[i], wr_lo[i], b_r))
            src, pos2, tile_expert, n_used = _moe_plan(route)
            xs = _gather_rows(src, h)
            ys = _moe_experts(tile_expert, n_used, xs, moe_w1[i], moe_w3[i], moe_w2[i])
            z = _gather_rows(pos2, ys).reshape(2, T_ALL, D)
            y = _moe_combine(z, route, y, mod_l, row1(g_post_ff, l))

    y_prompt = y[:T_CTX].reshape(N_CTX_B, CTX_L, D)
    y_sample = y[T_CTX:].reshape(N_LAT_B, LAT_L, D)
    stack = lambda xs, shp: jnp.stack([a.reshape((N_CTX_B, CTX_L) + shp) for a in xs], axis=1)
    return (y_prompt, y_sample, stack(st_k, (NA_H, NA_D)), stack(st_v, (NA_H, NA_D)),
            stack(st_ckv, (KV_RANK,)), stack(st_kr, (ROPE,)))
```

```python
import functools

import numpy as np
import jax
import jax.numpy as jnp
from jax import lax
from jax.experimental import pallas as pl
from jax.experimental.pallas import tpu as pltpu

F32 = jnp.float32
BF16 = jnp.bfloat16

D = 1024
DEPTH = 4
N_CTX_B, CTX_L = 16, 256
N_LAT_B, LAT_L = 2, 4096
GRID_W = 64
GRID_ROWS = LAT_L // GRID_W
GROUP = 4096
T_CTX = N_CTX_B * CTX_L
T_LAT = N_LAT_B * LAT_L
T_ALL = T_CTX + T_LAT
PAST = 256
CONV_W = 256
NA_H, NA_D = 6, 64
NA_W = NA_H * NA_D
WIN_R, WIN_C = 8, 16
RPB_R, RPB_C = 2 * WIN_R - 1, 2 * WIN_C - 1
MLA_H, NOPE, ROPE, VD = 6, 64, 32, 64
MLA_W = MLA_H * VD
Q_RANK, KV_RANK = 256, 128
IN_MAIN = 3 * CONV_W + 3 * NA_W + Q_RANK + KV_RANK
IN_PAD = IN_MAIN + 256
D_FF = 2816
N_EXP = 8
D_FFE = 3584
EPS = 1e-6
NEG = -1e30
ROPE_BASE = 10000.0
LANE = 128
VMEM_LIMIT = 56 * 1024 * 1024

TM = 512
MOE_TM = 512
MOE_TF = 512
MOE_ROWS = 2 * T_ALL + N_EXP * MOE_TM
MOE_TILES = MOE_ROWS // MOE_TM
GATHER_R = 256


def _rms(x, g):
    return x * lax.rsqrt(jnp.mean(x * x, axis=-1, keepdims=True) + EPS) * g


def _silu(x):
    return x * (1.0 / (1.0 + jnp.exp(-x)))


def _dot(a, b):
    return jnp.dot(a, b, preferred_element_type=F32)


def _dot_nt(a, b):
    return lax.dot_general(a, b, (((1,), (1,)), ((), ())), preferred_element_type=F32)


def _params(sem, vmem=VMEM_LIMIT, **kw):
    return pltpu.CompilerParams(dimension_semantics=sem, vmem_limit_bytes=vmem, **kw)


def _layer_spec(l, shape):
    nd = len(shape)
    return pl.BlockSpec((None,) + tuple(shape), lambda *_: (l,) + (0,) * nd, pipeline_mode=pl.Buffered(1))


def _mod_spec(l):
    return pl.BlockSpec((None, None, 6, D), lambda i: (l, i * TM // GROUP, 0, 0))


def _rows(w):
    return pl.BlockSpec((TM, w), lambda i: (i, 0))


MOD_TN = 1536


def _mod_kernel(cond_ref, w_ref, b_ref, o_ref):
    s = _silu(cond_ref[...]).astype(BF16)
    o_ref[0] = _dot(s, w_ref[0].astype(BF16)) + b_ref[0]


def _modulation(cond8, w_mod, b_mod):
    return pl.pallas_call(
        _mod_kernel,
        grid=(DEPTH, 6 * D // MOD_TN),
        in_specs=[
            pl.BlockSpec((8, D), lambda l, n: (0, 0)),
            pl.BlockSpec((1, D, MOD_TN), lambda l, n: (l, 0, n)),
            pl.BlockSpec((1, 1, MOD_TN), lambda l, n: (l, 0, n)),
        ],
        out_specs=pl.BlockSpec((1, 8, MOD_TN), lambda l, n: (l, 0, n)),
        out_shape=jax.ShapeDtypeStruct((DEPTH, 8, 6 * D), F32),
        compiler_params=_params(("arbitrary", "arbitrary")),
        name="adaln_mod",
    )(cond8, w_mod, b_mod.reshape(DEPTH, 1, 6 * D))


def _inproj_kernel(y_ref, mod_ref, gpre_ref, w_ref, gqa_ref, gkva_ref,
                   bu_ref, q_ref, k_ref, v_ref, qa_ref, ckv_ref, kr_ref):
    mod = mod_ref[...]
    h = _rms(y_ref[...], gpre_ref[...]) * (1.0 + mod[1:2]) + mod[0:1]
    z = _dot(h.astype(BF16), w_ref[...])
    bu_ref[:, 0:CONV_W] = z[:, 0:CONV_W]
    bu_ref[:, CONV_W:2 * CONV_W] = z[:, CONV_W:2 * CONV_W] * z[:, 2 * CONV_W:3 * CONV_W]
    o = 3 * CONV_W
    q_ref[...] = (z[:, o:o + NA_W] * (NA_D ** -0.5)).astype(BF16)
    k_ref[...] = z[:, o + NA_W:o + 2 * NA_W]
    v_ref[...] = z[:, o + 2 * NA_W:o + 3 * NA_W]
    o += 3 * NA_W
    qa_ref[...] = _rms(z[:, o:o + Q_RANK], gqa_ref[...]).astype(BF16)
    o += Q_RANK
    ckv_ref[...] = _rms(z[:, o:o + KV_RANK], gkva_ref[...])
    kr_ref[...] = z[:, IN_MAIN:IN_PAD]


def _in_projection(l, y, mods, g_pre, w_in_p, g_qa, g_kva):
    shp = lambda w, dt: jax.ShapeDtypeStruct((T_ALL, w), dt)
    return pl.pallas_call(
        _inproj_kernel,
        grid=(T_ALL // TM,),
        in_specs=[_rows(D), _mod_spec(l), _layer_spec(l, (1, D)), _layer_spec(l, (D, IN_PAD)),
                  _layer_spec(l, (1, Q_RANK)), _layer_spec(l, (1, KV_RANK))],
        out_specs=[_rows(2 * CONV_W), _rows(NA_W), _rows(NA_W), _rows(NA_W), _rows(Q_RANK), _rows(KV_RANK),
                   _rows(2 * LANE)],
        out_shape=[shp(2 * CONV_W, F32), shp(NA_W, BF16), shp(NA_W, F32), shp(NA_W, F32),
                   shp(Q_RANK, BF16), shp(KV_RANK, F32), shp(2 * LANE, F32)],
        compiler_params=_params(("arbitrary",)),
        name="prenorm_inproj",
    )(y, mods, g_pre, w_in_p, g_qa, g_kva)


def _lane_lo():
    return lax.broadcasted_iota(jnp.int32, (1, LANE), 1) < NA_D


def _pair_attention(q2, keys, vals, biases):
    lo = _lane_lo()
    outs = []
    for u in range(2):
        qm = jnp.where(lo if u == 0 else jnp.logical_not(lo), q2, jnp.zeros_like(q2))
        ss = []
        for kk, bb in zip(keys, biases):
            s = _dot_nt(qm, kk)
            if bb is not None:
                s = s + bb(u)
            ss.append(s)
        m = ss[0].max(axis=-1, keepdims=True)
        for s in ss[1:]:
            m = jnp.maximum(m, s.max(axis=-1, keepdims=True))
        acc = None
        l = None
        for s, vv in zip(ss, vals):
            p = jnp.exp(s - m)
            ls = p.sum(axis=-1, keepdims=True)
            o = _dot(p.astype(BF16), vv)
            acc = o if acc is None else acc + o
            l = ls if l is None else l + ls
        outs.append(acc * (1.0 / l))
    return jnp.where(lo, outs[0], outs[1])


def _ctx_attn_kernel(q_ref, k_ref, v_ref, qa_ref, ckv_ref, kr_ref, wq_ref, wk_ref, wv_ref, o_ref):
    for j in range(NA_H // 2):
        sl = slice(LANE * j, LANE * (j + 1))
        k2 = k_ref[:, sl].astype(BF16)
        v2 = v_ref[:, sl].astype(BF16)
        o_ref[:, sl] = _pair_attention(q_ref[:, sl], [k2], [v2], [None]).astype(BF16)
    qa = qa_ref[...]
    ckv = ckv_ref[...].astype(BF16)
    kr = kr_ref[:, 0:LANE]
    scale = (NOPE + ROPE) ** -0.5
    lo = _lane_lo()
    for j in range(MLA_H // 2):
        v2 = _dot(ckv, wv_ref[j]).astype(BF16)
        outs = []
        for u in range(2):
            h = 2 * j + u
            qh = (_dot(qa, wq_ref[h]) * scale).astype(BF16)
            kh = (_dot(ckv, wk_ref[h]) + kr).astype(BF16)
            s = _dot_nt(qh, kh)
            m = s.max(axis=-1, keepdims=True)
            p = jnp.exp(s - m)
            l = p.sum(axis=-1, keepdims=True)
            outs.append(_dot(p.astype(BF16), v2) * (1.0 / l))
        o_ref[:, NA_W + LANE * j:NA_W + LANE * (j + 1)] = jnp.where(lo, outs[0], outs[1]).astype(BF16)


def _ctx_attention(l, q, k, v, qa, ckv, kr, wq, wk, wv2):
    rows = lambda w: pl.BlockSpec((CTX_L, w), lambda b: (b, 0))
    return pl.pallas_call(
        _ctx_attn_kernel,
        grid=(N_CTX_B,),
        in_specs=[rows(NA_W), rows(NA_W), rows(NA_W), rows(Q_RANK), rows(KV_RANK), rows(2 * LANE),
                  _layer_spec(l, (MLA_H, Q_RANK, LANE)), _layer_spec(l, (MLA_H, KV_RANK, LANE)),
                  _layer_spec(l, (MLA_H // 2, KV_RANK, LANE))],
        out_specs=rows(NA_W + MLA_W),
        out_shape=jax.ShapeDtypeStruct((T_CTX, NA_W + MLA_W), BF16),
        compiler_params=_params(("arbitrary",)),
        name="ctx_attention",
    )(q, k, v, qa, ckv, kr, wq, wk, wv2)


NA_QROWS = 8
NA_KROWS = 16
NA_TQ = NA_QROWS * GRID_W
NA_TK = NA_KROWS * GRID_W
NA_RB = GRID_ROWS // NA_QROWS
NA_VARIANTS = ((0, 0), (NA_QROWS, NA_QROWS - WIN_R // 2), (GRID_ROWS - NA_QROWS, GRID_ROWS - NA_KROWS))


def _na_bias_kernel(rpb_ref, o_ref):
    l = pl.program_id(0)
    var = pl.program_id(1)
    h = pl.program_id(2)
    lane = lax.broadcasted_iota(jnp.int32, (GRID_W, LANE), 1)
    qc = lax.broadcasted_iota(jnp.int32, (GRID_W, LANE), 0)
    kc = jnp.bitwise_and(lane, GRID_W - 1)
    dcol = jnp.clip(kc - qc, -(WIN_C - 1), WIN_C - 1) + (WIN_C - 1)
    cstart = jnp.clip(qc - WIN_C // 2, 0, GRID_W - WIN_C)
    col_ok = jnp.logical_and(kc >= cstart, kc < cstart + WIN_C)
    lo = lane < GRID_W
    hits = [dcol == b for b in range(RPB_C)]
    tabs = []
    for a in range(RPB_R):
        acc = jnp.zeros((GRID_W, LANE), F32)
        for b in range(RPB_C):
            acc = jnp.where(hits[b], rpb_ref[((l * RPB_R + a) * RPB_C + b) * NA_H + h], acc)
        tabs.append(jnp.where(col_ok, acc, NEG))
    neg = jnp.full((GRID_W, LANE), NEG, F32)
    for vi, (r0, s0) in enumerate(NA_VARIANTS):
        @pl.when(var == vi)
        def _(r0=r0, s0=s0):
            for qi in range(NA_QROWS):
                qr = r0 + qi
                rs = min(max(qr - WIN_R // 2, 0), GRID_ROWS - WIN_R)
                for m in range(NA_KROWS // 2):
                    kl = s0 + 2 * m
                    left = tabs[kl - qr + WIN_R - 1] if rs <= kl < rs + WIN_R else neg
                    right = tabs[kl + 1 - qr + WIN_R - 1] if rs <= kl + 1 < rs + WIN_R else neg
                    blk = left if left is right else jnp.where(lo, left, right)
                    o_ref[GRID_W * qi:GRID_W * (qi + 1), LANE * m:LANE * (m + 1)] = blk


def _na_bias(rpb):
    return pl.pallas_call(
        _na_bias_kernel,
        grid=(DEPTH, 3, NA_H),
        in_specs=[pl.BlockSpec(memory_space=pltpu.SMEM)],
        out_specs=pl.BlockSpec((None, None, None, NA_TQ, NA_TK), lambda l, v, h: (l, v, h, 0, 0)),
        out_shape=jax.ShapeDtypeStruct((DEPTH, 3, NA_H, NA_TQ, NA_TK), F32),
        compiler_params=_params(("arbitrary", "arbitrary", "arbitrary")),
        name="na_bias_tiles",
    )(rpb.reshape(-1))


def _na_lat_kernel(q_ref, k_ref, v_ref, kc_ref, vc_ref, b_ref, o_ref):
    rb = pl.program_id(2)
    start = pl.multiple_of(jnp.clip(NA_QROWS * rb - WIN_R // 2, 0, GRID_ROWS - NA_KROWS) * GRID_W, 256)
    ks = k_ref[pl.ds(start, NA_TK), :].astype(BF16)
    vs = v_ref[pl.ds(start, NA_TK), :].astype(BF16)
    kc = kc_ref[...].astype(BF16)
    vc = vc_ref[...].astype(BF16)
    o = _pair_attention(q_ref[...], [ks, kc], [vs, vc], [lambda u: b_ref[u], None])
    o_ref[...] = o.astype(BF16)


def _na_latent(l, q, k3, v3, kc, vc, bias):
    def variant(rb):
        return jnp.where(rb == 0, 0, jnp.where(rb == NA_RB - 1, 2, 1))

    qoff = T_CTX // NA_TQ
    return pl.pallas_call(
        _na_lat_kernel,
        grid=(N_LAT_B, NA_H // 2, NA_RB),
        in_specs=[
            pl.BlockSpec((NA_TQ, LANE), lambda b, j, r: (qoff + b * NA_RB + r, j)),
            pl.BlockSpec((None, LAT_L, LANE), lambda b, j, r: (1 + b, 0, j)),
            pl.BlockSpec((None, LAT_L, LANE), lambda b, j, r: (1 + b, 0, j)),
            pl.BlockSpec((None, None, PAST, LANE), lambda b, j, r: (b, l, 0, j)),
            pl.BlockSpec((None, None, PAST, LANE), lambda b, j, r: (b, l, 0, j)),
            pl.BlockSpec((None, None, 2, NA_TQ, NA_TK), lambda b, j, r: (l, variant(r), j, 0, 0)),
        ],
        out_specs=pl.BlockSpec((NA_TQ, LANE), lambda b, j, r: (b * NA_RB + r, j)),
        out_shape=jax.ShapeDtypeStruct((T_LAT, NA_W), BF16),
        compiler_params=_params(("arbitrary", "arbitrary", "arbitrary")),
        name="na_latent",
    )(q, k3, v3, kc, vc, bias)


MLA_TP = 512


def _mla_prep_kernel(*refs, with_q):
    if with_q:
        (qa_ref, ckv_ref, kr_ref, cq_ref, sq_ref, ck_ref, sk_ref,
         wq_ref, wqr_ref, wk_ref, wv_ref, qo_ref, ko_ref, vo_ref) = refs
    else:
        ckv_ref, kr_ref, ck_ref, sk_ref, wk_ref, wv_ref, ko_ref, vo_ref = refs
    ckv = ckv_ref[...].astype(BF16)
    krope = kr_ref[:, 0:LANE] * ck_ref[...] + kr_ref[:, LANE:2 * LANE] * sk_ref[...]
    for h in range(MLA_H):
        ko_ref[h] = (_dot(ckv, wk_ref[h]) + krope).astype(BF16)
    for j in range(MLA_H // 2):
        vo_ref[j] = _dot(ckv, wv_ref[j]).astype(BF16)
    if with_q:
        qa = qa_ref[...]
        scale = (NOPE + ROPE) ** -0.5
        for h in range(MLA_H):
            qh = _dot(qa, wq_ref[h]) * cq_ref[...] + _dot(qa, wqr_ref[h]) * sq_ref[...]
            qo_ref[h] = (qh * scale).astype(BF16)


def _mla_prep_latent(l, qa, ckv, kr, cosq, sinq, cosk, sink, wq, wqr, wk, wv2):
    off = T_CTX // MLA_TP
    nblk = LAT_L // MLA_TP
    rows = lambda w: pl.BlockSpec((MLA_TP, w), lambda b, i: (off + b * nblk + i, 0))
    tab = pl.BlockSpec((MLA_TP, LANE), lambda b, i: (i, 0))
    outh = lambda n: pl.BlockSpec((None, n, MLA_TP, LANE), lambda b, i: (b, 0, i, 0))
    return pl.pallas_call(
        functools.partial(_mla_prep_kernel, with_q=True),
        grid=(N_LAT_B, nblk),
        in_specs=[rows(Q_RANK), rows(KV_RANK), rows(2 * LANE), tab, tab, tab, tab,
                  _layer_spec(l, (MLA_H, Q_RANK, LANE)), _layer_spec(l, (MLA_H, Q_RANK, LANE)),
                  _layer_spec(l, (MLA_H, KV_RANK, LANE)), _layer_spec(l, (MLA_H // 2, KV_RANK, LANE))],
        out_specs=[outh(MLA_H), outh(MLA_H), outh(MLA_H // 2)],
        out_shape=[jax.ShapeDtypeStruct((N_LAT_B, MLA_H, LAT_L, LANE), BF16),
                   jax.ShapeDtypeStruct((N_LAT_B, MLA_H, LAT_L, LANE), BF16),
                   jax.ShapeDtypeStruct((N_LAT_B, MLA_H // 2, LAT_L, LANE), BF16)],
        compiler_params=_params(("arbitrary", "arbitrary")),
        name="mla_prep_latent",
    )(qa, ckv, kr, cosq, sinq, cosk, sink, wq, wqr, wk, wv2)


def _mla_prep_cache(l, ckv, kr, ones, zeros, wk, wv2):
    rows = lambda w: pl.BlockSpec((None, None, PAST, w), lambda b: (b, l, 0, 0))
    tab = pl.BlockSpec((PAST, LANE), lambda b: (0, 0))
    outh = lambda n: pl.BlockSpec((None, n, PAST, LANE), lambda b: (b, 0, 0, 0))
    return pl.pallas_call(
        functools.partial(_mla_prep_kernel, with_q=False),
        grid=(N_LAT_B,),
        in_specs=[rows(KV_RANK), rows(2 * LANE), tab, tab,
                  _layer_spec(l, (MLA_H, KV_RANK, LANE)), _layer_spec(l, (MLA_H // 2, KV_RANK, LANE))],
        out_specs=[outh(MLA_H), outh(MLA_H // 2)],
        out_shape=[jax.ShapeDtypeStruct((N_LAT_B, MLA_H, PAST, LANE), BF16),
                   jax.ShapeDtypeStruct((N_LAT_B, MLA_H // 2, PAST, LANE), BF16)],
        compiler_params=_params(("arbitrary",)),
        name="mla_prep_cache",
    )(ckv, kr, ones, zeros, wk, wv2)


MLA_TQ = 256


def _mla_attn_kernel(q_ref, k_ref, kc_ref, v_ref, vc_ref, o_ref):
    lo = _lane_lo()
    v2 = v_ref[0]
    vc2 = vc_ref[0]
    outs = []
    for u in range(2):
        qh = q_ref[u]
        s = _dot_nt(qh, k_ref[u])
        sc = _dot_nt(qh, kc_ref[u])
        m = jnp.maximum(s.max(axis=-1, keepdims=True), sc.max(axis=-1, keepdims=True))
        p = jnp.exp(s - m)
        pc = jnp.exp(sc - m)
        l = p.sum(axis=-1, keepdims=True) + pc.sum(axis=-1, keepdims=True)
        o = _dot(p.astype(BF16), v2) + _dot(pc.astype(BF16), vc2)
        outs.append(o * (1.0 / l))
    o_ref[...] = jnp.where(lo, outs[0], outs[1]).astype(BF16)


def _mla_attention(qh, kh, kch, vh, vch):
    nq = LAT_L // MLA_TQ
    return pl.pallas_call(
        _mla_attn_kernel,
        grid=(N_LAT_B, MLA_H // 2, nq),
        in_specs=[
            pl.BlockSpec((None, 2, MLA_TQ, LANE), lambda b, j, i: (b, j, i, 0)),
            pl.BlockSpec((None, 2, LAT_L, LANE), lambda b, j, i: (b, j, 0, 0)),
            pl.BlockSpec((None, 2, PAST, LANE), lambda b, j, i: (b, j, 0, 0)),
            pl.BlockSpec((None, 1, LAT_L, LANE), lambda b, j, i: (b, j, 0, 0)),
            pl.BlockSpec((None, 1, PAST, LANE), lambda b, j, i: (b, j, 0, 0)),
        ],
        out_specs=pl.BlockSpec((MLA_TQ, LANE), lambda b, j, i: (b * nq + i, j)),
        out_shape=jax.ShapeDtypeStruct((T_LAT, MLA_W), BF16),
        compiler_params=_params(("arbitrary", "arbitrary", "arbitrary")),
        name="mla_attention",
    )(qh, kh, kch, vh, vch)


N_CTX_TILES = T_CTX // TM


def _merge_kernel(*refs, moe):
    if moe:
        (bu_ref, up_ref, un_ref, attc_ref, na_ref, mla_ref, y_ref, mod_ref, wconv_ref, ggrp_ref, wout_ref, gpost_ref,
         gpre_ref, wr_hi_ref, wr_lo_ref, br_ref, yo_ref, h_ref, route_ref) = refs
    else:
        (bu_ref, up_ref, un_ref, attc_ref, na_ref, mla_ref, y_ref, mod_ref, wconv_ref, ggrp_ref, wout_ref, gpost_ref,
         gpre_ref, yo_ref, h_ref) = refs
    i = pl.program_id(0)
    local = lax.broadcasted_iota(jnp.int32, (TM, 1), 0)
    row = i * TM + local
    seq_mask = jnp.where(row < T_CTX, CTX_L - 1, LAT_L - 1)
    pos = jnp.bitwise_and(row, seq_mask)
    u = bu_ref[:, CONV_W:2 * CONV_W]
    prev = jnp.where(local == 0, up_ref[7:8, :], pltpu.roll(u, 1, axis=0))
    prev = jnp.where(pos == 0, 0.0, prev)
    nxt = jnp.where(local == TM - 1, un_ref[0:1, :], pltpu.roll(u, TM - 1, axis=0))
    nxt = jnp.where(pos == seq_mask, 0.0, nxt)
    wc = wconv_ref[...]
    conv = bu_ref[:, 0:CONV_W] * (wc[0:1] * prev + wc[1:2] * u + wc[2:3] * nxt)
    gg = ggrp_ref[...]
    is_ctx = i < N_CTX_TILES
    att_na = jnp.where(is_ctx, attc_ref[:, 0:NA_W], na_ref[...]).astype(F32)
    att_mla = jnp.where(is_ctx, attc_ref[:, NA_W:], mla_ref[...]).astype(F32)
    mixin = jnp.concatenate([
        _rms(conv, gg[:, 0:CONV_W]).astype(BF16),
        _rms(att_na, gg[:, CONV_W:CONV_W + NA_W]).astype(BF16),
        _rms(att_mla, gg[:, CONV_W + NA_W:]).astype(BF16)], axis=-1)
    mix = _dot(mixin, wout_ref[...])
    mod = mod_ref[...]
    y = y_ref[...] + mod[2:3] * _rms(mix, gpost_ref[...])
    yo_ref[...] = y
    h = _rms(y, gpre_ref[...]) * (1.0 + mod[4:5]) + mod[3:4]
    if not moe:
        h_ref[...] = h.astype(BF16)
        return
    h_ref[...] = h
    hi = h.astype(BF16)
    lo = (h - hi.astype(F32)).astype(BF16)
    logits = _dot(hi, wr_hi_ref[...]) + (_dot(lo, wr_hi_ref[...]) + _dot(hi, wr_lo_ref[...])) + br_ref[...]
    lane = lax.broadcasted_iota(jnp.int32, (TM, LANE), 1).astype(F32)
    logits = jnp.where(lane < N_EXP, logits, NEG)
    m1 = logits.max(axis=-1, keepdims=True)
    i1 = jnp.where(logits == m1, lane, float(LANE)).min(axis=-1, keepdims=True)
    rest = jnp.where(lane == i1, NEG, logits)
    m2 = rest.max(axis=-1, keepdims=True)
    i2 = jnp.where(rest == m2, lane, float(LANE)).min(axis=-1, keepdims=True)
    e2 = jnp.exp(m2 - m1)
    g1 = 1.0 / (1.0 + e2)
    g2 = e2 * g1
    route_ref[...] = jnp.where(lane == 0, i1, jnp.where(lane == 1, i2, jnp.where(lane == 2, g1, jnp.where(lane == 3, g2, 0.0))))


def _merge(l, bu, att_ctx, na_lat, mla_lat, y, mods, w_conv, g_grp, w_out, g_post, g_pre_ff, router=None):
    moe = router is not None
    nb8 = TM // 8
    last8 = T_ALL // 8 - 1
    lat_rows = lambda w: pl.BlockSpec((TM, w), lambda i: (jnp.maximum(i - N_CTX_TILES, 0), 0))
    in_specs = [
        _rows(2 * CONV_W),
        pl.BlockSpec((8, CONV_W), lambda i: (jnp.maximum(i * nb8 - 1, 0), 1)),
        pl.BlockSpec((8, CONV_W), lambda i: (jnp.minimum((i + 1) * nb8, last8), 1)),
        pl.BlockSpec((TM, NA_W + MLA_W), lambda i: (jnp.minimum(i, N_CTX_TILES - 1), 0)),
        lat_rows(NA_W),
        lat_rows(MLA_W),
        _rows(D),
        _mod_spec(l),
        _layer_spec(l, (3, CONV_W)),
        _layer_spec(l, (1, D)),
        _layer_spec(l, (D, D)),
        _layer_spec(l, (1, D)),
        _layer_spec(l, (1, D)),
    ]
    args = [bu, bu, bu, att_ctx, na_lat, mla_lat, y, mods, w_conv, g_grp, w_out, g_post, g_pre_ff]
    out_specs = [_rows(D), _rows(D)]
    out_shape = [jax.ShapeDtypeStruct((T_ALL, D), F32), jax.ShapeDtypeStruct((T_ALL, D), F32 if moe else BF16)]
    if moe:
        li = l // 2
        in_specs += [_layer_spec(li, (D, LANE)), _layer_spec(li, (D, LANE)), _layer_spec(li, (1, LANE))]
        args += list(router)
        out_specs.append(_rows(LANE))
        out_shape.append(jax.ShapeDtypeStruct((T_ALL, LANE), F32))
    return pl.pallas_call(
        functools.partial(_merge_kernel, moe=moe),
        grid=(T_ALL // TM,),
        in_specs=in_specs,
        out_specs=out_specs,
        out_shape=out_shape,
        compiler_params=_params(("arbitrary",)),
        name="merge_outproj_moe" if moe else "merge_outproj",
    )(*args)


FF_CHUNK = D_FF // 2


def _ffn_kernel(h_ref, y_ref, mod_ref, w1_ref, w3_ref, w2_ref, gpost_ref, o_ref):
    h = h_ref[...]
    acc = None
    for c in range(D_FF // FF_CHUNK):
        sl = slice(c * FF_CHUNK, (c + 1) * FF_CHUNK)
        a1 = _dot(h, w1_ref[:, sl])
        a3 = _dot(h, w3_ref[:, sl])
        o = _dot((_silu(a1) * a3).astype(BF16), w2_ref[sl, :])
        acc = o if acc is None else acc + o
    o_ref[...] = y_ref[...] + mod_ref[5:6] * _rms(acc, gpost_ref[...])


def _dense_ffn(l, h, y, mods, w1, w3, w2, g_post):
    li = l // 2
    return pl.pallas_call(
        _ffn_kernel,
        grid=(T_ALL // TM,),
        in_specs=[_rows(D), _rows(D), _mod_spec(l), _layer_spec(li, (D, D_FF)), _layer_spec(li, (D, D_FF)),
                  _layer_spec(li, (D_FF, D)), _layer_spec(l, (1, D))],
        out_specs=_rows(D),
        out_shape=jax.ShapeDtypeStruct((T_ALL, D), F32),
        compiler_params=_params(("arbitrary",)),
        name="dense_ffn",
    )(h, y, mods, w1, w3, w2, g_post)


def _gather_kernel(idx_ref, src_ref, o_ref, sem):
    base = pl.program_id(0) * GATHER_R

    def row_copy(r):
        return pltpu.make_async_copy(src_ref.at[pl.ds(idx_ref[base + r], 1), :], o_ref.at[pl.ds(r, 1), :], sem)

    for r in range(GATHER_R):
        row_copy(r).start()
    for r in range(GATHER_R):
        row_copy(r).wait()


def _gather_rows(idx, src):
    n = idx.shape[0]
    return pl.pallas_call(
        _gather_kernel,
        grid_spec=pltpu.PrefetchScalarGridSpec(
            num_scalar_prefetch=1,
            grid=(n // GATHER_R,),
            in_specs=[pl.BlockSpec(memory_space=pl.ANY)],
            out_specs=pl.BlockSpec((GATHER_R, D), lambda i, idx: (i, 0)),
            scratch_shapes=[pltpu.SemaphoreType.DMA(())],
        ),
        out_shape=jax.ShapeDtypeStruct((n, D), F32),
        compiler_params=_params(("arbitrary",), disable_bounds_checks=True),
        name="gather_rows",
    )(idx, src)


def _moe_kernel(te_ref, nused_ref, x_ref, w1_ref, w3_ref, w2_ref, o_ref, acc_ref):
    i = pl.program_id(0)
    j = pl.program_id(1)
    nj = pl.num_programs(1)
    used = i < nused_ref[0]

    @pl.when(jnp.logical_and(used, j == 0))
    def _():
        acc_ref[...] = jnp.zeros_like(acc_ref)

    @pl.when(used)
    def _():
        x = x_ref[...].astype(BF16)
        a1 = _dot(x, w1_ref[...].astype(BF16))
        a3 = _dot(x, w3_ref[...].astype(BF16))
        acc_ref[...] += _dot((_silu(a1) * a3).astype(BF16), w2_ref[...].astype(BF16))

    @pl.when(jnp.logical_and(used, j == nj - 1))
    def _():
        o_ref[...] = acc_ref[...]

    @pl.when(jnp.logical_and(jnp.logical_not(used), j == nj - 1))
    def _():
        o_ref[...] = jnp.zeros_like(o_ref)


def _moe_experts(li, tile_expert, n_used, xs, w1, w3, w2):
    nj = D_FFE // MOE_TF

    def jj(i, j, nu):
        return jnp.where(i < nu[0], j, nj - 1)

    return pl.pallas_call(
        _moe_kernel,
        grid_spec=pltpu.PrefetchScalarGridSpec(
            num_scalar_prefetch=2,
            grid=(MOE_TILES, nj),
            in_specs=[
                pl.BlockSpec((MOE_TM, D), lambda i, j, te, nu: (i, 0)),
                pl.BlockSpec((None, None, D, MOE_TF), lambda i, j, te, nu: (li, te[i], 0, jj(i, j, nu))),
                pl.BlockSpec((None, None, D, MOE_TF), lambda i, j, te, nu: (li, te[i], 0, jj(i, j, nu))),
                pl.BlockSpec((None, None, MOE_TF, D), lambda i, j, te, nu: (li, te[i], jj(i, j, nu), 0)),
            ],
            out_specs=pl.BlockSpec((MOE_TM, D), lambda i, j, te, nu: (i, 0)),
            scratch_shapes=[pltpu.VMEM((MOE_TM, D), F32)],
        ),
        out_shape=jax.ShapeDtypeStruct((MOE_ROWS, D), F32),
        compiler_params=_params(("arbitrary", "arbitrary")),
        name="moe_experts",
    )(tile_expert, n_used, xs, w1, w3, w2)


def _moe_combine_kernel(z_ref, route_ref, y_ref, mod_ref, gpost_ref, o_ref):
    r = route_ref[...]
    f = r[:, 2:3] * z_ref[0] + r[:, 3:4] * z_ref[1]
    o_ref[...] = y_ref[...] + mod_ref[5:6] * _rms(f, gpost_ref[...])


def _moe_combine(l, z, route, y, mods, g_post):
    return pl.pallas_call(
        _moe_combine_kernel,
        grid=(T_ALL // TM,),
        in_specs=[pl.BlockSpec((2, TM, D), lambda i: (0, i, 0)), _rows(LANE), _rows(D), _mod_spec(l),
                  _layer_spec(l, (1, D))],
        out_specs=_rows(D),
        out_shape=jax.ShapeDtypeStruct((T_ALL, D), F32),
        compiler_params=_params(("arbitrary",)),
        name="moe_combine",
    )(z, route, y, mods, g_post)


def _moe_plan(route):
    e = route[:, 0:2].astype(jnp.int32)
    onehot = (e.reshape(-1)[:, None] == jnp.arange(N_EXP, dtype=jnp.int32)[None, :]).astype(jnp.int32)
    csum = jnp.cumsum(onehot, axis=0)
    counts = csum[-1]
    rank = jnp.sum((csum - 1) * onehot, axis=1)
    padded = (counts + MOE_TM - 1) // MOE_TM * MOE_TM
    ends = jnp.cumsum(padded)
    starts = ends - padded
    pos = jnp.sum(starts[None, :] * onehot, axis=1) + rank
    token = jnp.arange(2 * T_ALL, dtype=jnp.int32) // 2
    src = jnp.zeros((MOE_ROWS,), jnp.int32).at[pos].set(token)
    tile_start = jnp.arange(MOE_TILES, dtype=jnp.int32) * MOE_TM
    n_used = (ends[-1] // MOE_TM).astype(jnp.int32)
    tile_expert = jnp.sum((tile_start[:, None] >= ends[None, :]).astype(jnp.int32), axis=1)
    last = jnp.minimum(tile_expert[jnp.maximum(n_used - 1, 0)], N_EXP - 1)
    tile_expert = jnp.where(jnp.arange(MOE_TILES) < n_used, jnp.minimum(tile_expert, N_EXP - 1), last).astype(jnp.int32)
    pos2 = pos.reshape(T_ALL, 2).T.reshape(-1)
    return src, pos2.astype(jnp.int32), tile_expert, n_used.reshape(1)


def _rope_tables():
    half, quarter = ROPE // 2, ROPE // 4
    t = np.arange(LAT_L)
    pos = np.stack([t // GRID_W, t % GRID_W], axis=-1).astype(np.float32)
    inv_freq = np.power(np.float32(ROPE_BASE), -np.arange(quarter, dtype=np.float32) * np.float32(2.0) / np.float32(half))
    ang = pos[:, :, None] * inv_freq.astype(np.float32)
    ang = np.concatenate([ang, ang], axis=-1).reshape(LAT_L, ROPE).astype(np.float32)
    cos, sin = np.cos(ang).astype(np.float32), np.sin(ang).astype(np.float32)
    z64, z32 = np.zeros((LAT_L, NOPE), np.float32), np.zeros((LAT_L, LANE - NOPE - ROPE), np.float32)
    cosq = np.concatenate([np.ones((LAT_L, NOPE), np.float32), cos, z32], axis=1)
    cosk = np.concatenate([z64, cos, z32], axis=1)
    sin_t = np.concatenate([z64, sin, z32], axis=1)
    return jnp.asarray(cosq), jnp.asarray(sin_t), jnp.asarray(cosk), jnp.asarray(sin_t)


def _rot_cols(w):
    k = w.shape[:-1]
    w4 = w.reshape(k + (2, 2, ROPE // 4))
    return jnp.concatenate([-w4[..., 1:2, :], w4[..., 0:1, :]], axis=-2).reshape(k + (ROPE,))


def _prep_weights(w_in, w_qb, w_kvb, w_router):
    zeros = lambda *s: jnp.zeros(s, F32)
    w_kr = w_in[:, :, IN_MAIN:]
    w_in_p = jnp.concatenate([
        w_in[:, :, :IN_MAIN],
        zeros(DEPTH, D, NOPE), w_kr, zeros(DEPTH, D, LANE - NOPE - ROPE),
        zeros(DEPTH, D, NOPE), _rot_cols(w_kr), zeros(DEPTH, D, LANE - NOPE - ROPE)], axis=-1).astype(BF16)
    wq4 = w_qb.reshape(DEPTH, Q_RANK, MLA_H, NOPE + ROPE).transpose(0, 2, 1, 3)
    pad_q = zeros(DEPTH, MLA_H, Q_RANK, LANE - NOPE - ROPE)
    wq = jnp.concatenate([wq4, pad_q], axis=-1).astype(BF16)
    wqr = jnp.concatenate([zeros(DEPTH, MLA_H, Q_RANK, NOPE), _rot_cols(wq4[..., NOPE:]), pad_q], axis=-1).astype(BF16)
    wkv4 = w_kvb.reshape(DEPTH, KV_RANK, MLA_H, NOPE + VD).transpose(0, 2, 1, 3)
    wk = jnp.concatenate([wkv4[..., :NOPE], zeros(DEPTH, MLA_H, KV_RANK, LANE - NOPE)], axis=-1).astype(BF16)
    wv = wkv4[..., NOPE:]
    wv2 = jnp.concatenate([wv[:, 0::2], wv[:, 1::2]], axis=-1).astype(BF16)
    wr = jnp.pad(w_router, ((0, 0), (0, 0), (0, LANE - N_EXP)))
    wr_hi = wr.astype(BF16)
    wr_lo = (wr - wr_hi.astype(F32)).astype(BF16)
    return w_in_p, wq, wqr, wk, wv2, wr_hi, wr_lo


def kernel(x_prompt, x_sample, cache_na_k, cache_na_v, cache_mla_ckv, cache_mla_krope, c, c_ctx,
           w_in, w_conv, rpb, g_qa, w_qb, g_kva, w_kvb, g_grp, w_out, w_mod, b_mod,
           g_pre_mix, g_post_mix, g_pre_ff, g_post_ff, w_ff1, w_ff3, w_ff2,
           w_router, b_router, moe_w1, moe_w3, moe_w2):
    y = jnp.concatenate([x_prompt.reshape(T_CTX, D), x_sample.reshape(T_LAT, D)], axis=0)
    cond8 = jnp.concatenate([c_ctx[None, :], c, jnp.zeros((5, D), F32)], axis=0)
    mods = _modulation(cond8, w_mod, b_mod)[:, :3].reshape(DEPTH, 3, 6, D)

    w_in_p, wq, wqr, wk, wv2, wr_hi, wr_lo = _prep_weights(w_in, w_qb, w_kvb, w_router)
    w_out_b = w_out.astype(BF16)
    ff1, ff3, ff2 = w_ff1.astype(BF16), w_ff3.astype(BF16), w_ff2.astype(BF16)
    b_r = jnp.pad(b_router, ((0, 0), (0, LANE - N_EXP)))[:, None, :]
    vec = lambda a: a[:, None, :]
    g_pre_mix, g_post_mix, g_pre_ff, g_post_ff = vec(g_pre_mix), vec(g_post_mix), vec(g_pre_ff), vec(g_post_ff)
    g_qa, g_kva, g_grp = vec(g_qa), vec(g_kva), vec(g_grp)
    cosq, sinq, cosk, sink = _rope_tables()
    ones_t = jnp.ones((PAST, LANE), F32)
    zeros_t = jnp.zeros((PAST, LANE), F32)
    bias = _na_bias(rpb)
    kc = cache_na_k.reshape(N_LAT_B, DEPTH, PAST, NA_W)
    vc = cache_na_v.reshape(N_LAT_B, DEPTH, PAST, NA_W)
    c_kr = jnp.pad(cache_mla_krope, ((0, 0), (0, 0), (0, 0), (NOPE, 2 * LANE - NOPE - ROPE)))

    st_k, st_v, st_ckv, st_kr = [], [], [], []
    for l in range(DEPTH):
        bu, q, k, v, qa, ckv, kr = _in_projection(l, y, mods, g_pre_mix, w_in_p, g_qa, g_kva)
        st_k.append(k[:T_CTX])
        st_v.append(v[:T_CTX])
        st_ckv.append(ckv[:T_CTX])
        st_kr.append(kr[:T_CTX, NOPE:NOPE + ROPE])

        att_ctx = _ctx_attention(l, q, k, v, qa, ckv, kr, wq, wk, wv2)
        na_lat = _na_latent(l, q, k.reshape(3, GROUP, NA_W), v.reshape(3, GROUP, NA_W), kc, vc, bias)
        qh, kh, vh = _mla_prep_latent(l, qa, ckv, kr, cosq, sinq, cosk, sink, wq, wqr, wk, wv2)
        kch, vch = _mla_prep_cache(l, cache_mla_ckv, c_kr, ones_t, zeros_t, wk, wv2)
        mla_lat = _mla_attention(qh, kh, kch, vh, vch)

        li = l // 2
        if l % 2 == 0:
            y, h = _merge(l, bu, att_ctx, na_lat, mla_lat, y, mods, w_conv, g_grp, w_out_b, g_post_mix, g_pre_ff)
            y = _dense_ffn(l, h, y, mods, ff1, ff3, ff2, g_post_ff)
        else:
            y, h, route = _merge(l, bu, att_ctx, na_lat, mla_lat, y, mods, w_conv, g_grp, w_out_b, g_post_mix, g_pre_ff,
                                 router=(wr_hi, wr_lo, b_r))
            src, pos2, tile_expert, n_used = _moe_plan(route)
            xs = _gather_rows(src, h)
            ys = _moe_experts(li, tile_expert, n_used, xs, moe_w1, moe_w3, moe_w2)
            z = _gather_rows(pos2, ys).reshape(2, T_ALL, D)
            y = _moe_combine(l, z, route, y, mods, g_post_ff)

    y_prompt = y[:T_CTX].reshape(N_CTX_B, CTX_L, D)
    y_sample = y[T_CTX:].reshape(N_LAT_B, LAT_L, D)
    stack = lambda xs, shp: jnp.stack([a.reshape((N_CTX_B, CTX_L) + shp) for a in xs], axis=1)
    return (y_prompt, y_sample, stack(st_k, (NA_H, NA_D)), stack(st_v, (NA_H, NA_D)),
            stack(st_ckv, (KV_RANK,)), stack(st_kr, (ROPE,)))
```

```python
import functools

import numpy as np
import jax
import jax.numpy as jnp
from jax import lax
from jax.experimental import pallas as pl
from jax.experimental.pallas import tpu as pltpu

F32 = jnp.float32
BF16 = jnp.bfloat16

D = 1024
DEPTH = 4
N_CTX_B, CTX_L = 16, 256
N_LAT_B, LAT_L = 2, 4096
GRID_W = 64
GRID_ROWS = LAT_L // GRID_W
GROUP = 4096
T_CTX = N_CTX_B * CTX_L
T_LAT = N_LAT_B * LAT_L
T_ALL = T_CTX + T_LAT
PAST = 256
CONV_W = 256
NA_H, NA_D = 6, 64
NA_W = NA_H * NA_D
WIN_R, WIN_C = 8, 16
RPB_R, RPB_C = 2 * WIN_R - 1, 2 * WIN_C - 1
MLA_H, NOPE, ROPE, VD = 6, 64, 32, 64
MLA_W = MLA_H * VD
Q_RANK, KV_RANK = 256, 128
IN_MAIN = 3 * CONV_W + 3 * NA_W + Q_RANK + KV_RANK
IN_PAD = IN_MAIN + 256
D_FF = 2816
N_EXP = 8
D_FFE = 3584
EPS = 1e-6
NEG = -1e30
ROPE_BASE = 10000.0
LANE = 128
VMEM_LIMIT = 56 * 1024 * 1024

TM = 512
MOE_U = 1024
MOE_SUB = 256
MOE_TF = 896
MOE_UNITS = 2 * T_ALL // MOE_U + N_EXP
MOE_ASG_LEN = 2 * T_ALL + MOE_U
MOE_Z_ROWS = 2 * T_ALL


def _rms(x, g):
    return x * lax.rsqrt(jnp.mean(x * x, axis=-1, keepdims=True) + EPS) * g


def _silu(x):
    return x * (1.0 / (1.0 + jnp.exp(-x)))


def _dot(a, b):
    return jnp.dot(a, b, preferred_element_type=F32)


def _dot_nt(a, b):
    return lax.dot_general(a, b, (((1,), (1,)), ((), ())), preferred_element_type=F32)


def _params(sem, vmem=VMEM_LIMIT, **kw):
    return pltpu.CompilerParams(dimension_semantics=sem, vmem_limit_bytes=vmem, **kw)


def _layer_spec(l, shape):
    nd = len(shape)
    return pl.BlockSpec((None,) + tuple(shape), lambda *_: (l,) + (0,) * nd, pipeline_mode=pl.Buffered(1))


def _mod_spec(l):
    return pl.BlockSpec((None, None, 6, D), lambda i: (l, i * TM // GROUP, 0, 0))


def _rows(w):
    return pl.BlockSpec((TM, w), lambda i: (i, 0))


MOD_TN = 1536


def _mod_kernel(cond_ref, w_ref, b_ref, o_ref):
    s = _silu(cond_ref[...]).astype(BF16)
    o_ref[0] = _dot(s, w_ref[0].astype(BF16)) + b_ref[0]


def _modulation(cond8, w_mod, b_mod):
    return pl.pallas_call(
        _mod_kernel,
        grid=(DEPTH, 6 * D // MOD_TN),
        in_specs=[
            pl.BlockSpec((8, D), lambda l, n: (0, 0)),
            pl.BlockSpec((1, D, MOD_TN), lambda l, n: (l, 0, n)),
            pl.BlockSpec((1, 1, MOD_TN), lambda l, n: (l, 0, n)),
        ],
        out_specs=pl.BlockSpec((1, 8, MOD_TN), lambda l, n: (l, 0, n)),
        out_shape=jax.ShapeDtypeStruct((DEPTH, 8, 6 * D), F32),
        compiler_params=_params(("arbitrary", "arbitrary")),
        name="adaln_mod",
    )(cond8, w_mod, b_mod.reshape(DEPTH, 1, 6 * D))


def _inproj_kernel(y_ref, mod_ref, gpre_ref, w_ref, gqa_ref, gkva_ref,
                   bu_ref, q_ref, k_ref, v_ref, qa_ref, ckv_ref, kr_ref):
    mod = mod_ref[...]
    h = _rms(y_ref[...], gpre_ref[...]) * (1.0 + mod[1:2]) + mod[0:1]
    z = _dot(h.astype(BF16), w_ref[...])
    bu_ref[:, 0:CONV_W] = z[:, 0:CONV_W]
    bu_ref[:, CONV_W:2 * CONV_W] = z[:, CONV_W:2 * CONV_W] * z[:, 2 * CONV_W:3 * CONV_W]
    o = 3 * CONV_W
    q_ref[...] = (z[:, o:o + NA_W] * (NA_D ** -0.5)).astype(BF16)
    k_ref[...] = z[:, o + NA_W:o + 2 * NA_W]
    v_ref[...] = z[:, o + 2 * NA_W:o + 3 * NA_W]
    o += 3 * NA_W
    qa_ref[...] = _rms(z[:, o:o + Q_RANK], gqa_ref[...]).astype(BF16)
    o += Q_RANK
    ckv_ref[...] = _rms(z[:, o:o + KV_RANK], gkva_ref[...])
    kr_ref[...] = z[:, IN_MAIN:IN_PAD]


def _in_projection(l, y, mods, g_pre, w_in_p, g_qa, g_kva):
    shp = lambda w, dt: jax.ShapeDtypeStruct((T_ALL, w), dt)
    return pl.pallas_call(
        _inproj_kernel,
        grid=(T_ALL // TM,),
        in_specs=[_rows(D), _mod_spec(l), _layer_spec(l, (1, D)), _layer_spec(l, (D, IN_PAD)),
                  _layer_spec(l, (1, Q_RANK)), _layer_spec(l, (1, KV_RANK))],
        out_specs=[_rows(2 * CONV_W), _rows(NA_W), _rows(NA_W), _rows(NA_W), _rows(Q_RANK), _rows(KV_RANK),
                   _rows(2 * LANE)],
        out_shape=[shp(2 * CONV_W, F32), shp(NA_W, BF16), shp(NA_W, F32), shp(NA_W, F32),
                   shp(Q_RANK, BF16), shp(KV_RANK, F32), shp(2 * LANE, F32)],
        compiler_params=_params(("arbitrary",)),
        name="prenorm_inproj",
    )(y, mods, g_pre, w_in_p, g_qa, g_kva)


def _lane_lo():
    return lax.broadcasted_iota(jnp.int32, (1, LANE), 1) < NA_D


def _pair_attention(q2, keys, vals, biases):
    lo = _lane_lo()
    outs = []
    for u in range(2):
        qm = jnp.where(lo if u == 0 else jnp.logical_not(lo), q2, jnp.zeros_like(q2))
        ss = []
        for kk, bb in zip(keys, biases):
            s = _dot_nt(qm, kk)
            if bb is not None:
                s = s + bb(u)
            ss.append(s)
        m = ss[0].max(axis=-1, keepdims=True)
        for s in ss[1:]:
            m = jnp.maximum(m, s.max(axis=-1, keepdims=True))
        acc = None
        l = None
        for s, vv in zip(ss, vals):
            p = jnp.exp(s - m)
            ls = p.sum(axis=-1, keepdims=True)
            o = _dot(p.astype(BF16), vv)
            acc = o if acc is None else acc + o
            l = ls if l is None else l + ls
        outs.append(acc * (1.0 / l))
    return jnp.where(lo, outs[0], outs[1])


def _ctx_attn_kernel(q_ref, k_ref, v_ref, qa_ref, ckv_ref, kr_ref, wq_ref, wk_ref, wv_ref, o_ref):
    for j in range(NA_H // 2):
        sl = slice(LANE * j, LANE * (j + 1))
        k2 = k_ref[:, sl].astype(BF16)
        v2 = v_ref[:, sl].astype(BF16)
        o_ref[:, sl] = _pair_attention(q_ref[:, sl], [k2], [v2], [None]).astype(BF16)
    qa = qa_ref[...]
    ckv = ckv_ref[...].astype(BF16)
    kr = kr_ref[:, 0:LANE]
    scale = (NOPE + ROPE) ** -0.5
    lo = _lane_lo()
    for j in range(MLA_H // 2):
        v2 = _dot(ckv, wv_ref[j]).astype(BF16)
        outs = []
        for u in range(2):
            h = 2 * j + u
            qh = (_dot(qa, wq_ref[h]) * scale).astype(BF16)
            kh = (_dot(ckv, wk_ref[h]) + kr).astype(BF16)
            s = _dot_nt(qh, kh)
            m = s.max(axis=-1, keepdims=True)
            p = jnp.exp(s - m)
            l = p.sum(axis=-1, keepdims=True)
            outs.append(_dot(p.astype(BF16), v2) * (1.0 / l))
        o_ref[:, NA_W + LANE * j:NA_W + LANE * (j + 1)] = jnp.where(lo, outs[0], outs[1]).astype(BF16)


def _ctx_attention(l, q, k, v, qa, ckv, kr, wq, wk, wv2):
    rows = lambda w: pl.BlockSpec((CTX_L, w), lambda b: (b, 0))
    return pl.pallas_call(
        _ctx_attn_kernel,
        grid=(N_CTX_B,),
        in_specs=[rows(NA_W), rows(NA_W), rows(NA_W), rows(Q_RANK), rows(KV_RANK), rows(2 * LANE),
                  _layer_spec(l, (MLA_H, Q_RANK, LANE)), _layer_spec(l, (MLA_H, KV_RANK, LANE)),
                  _layer_spec(l, (MLA_H // 2, KV_RANK, LANE))],
        out_specs=rows(NA_W + MLA_W),
        out_shape=jax.ShapeDtypeStruct((T_CTX, NA_W + MLA_W), BF16),
        compiler_params=_params(("arbitrary",)),
        name="ctx_attention",
    )(q, k, v, qa, ckv, kr, wq, wk, wv2)


NA_QROWS = 8
NA_KROWS = 16
NA_TQ = NA_QROWS * GRID_W
NA_TK = NA_KROWS * GRID_W
NA_RB = GRID_ROWS // NA_QROWS
NA_VARIANTS = ((0, 0), (NA_QROWS, NA_QROWS - WIN_R // 2), (GRID_ROWS - NA_QROWS, GRID_ROWS - NA_KROWS))


def _na_bias_kernel(rpb_ref, o_ref):
    l = pl.program_id(0)
    var = pl.program_id(1)
    h = pl.program_id(2)
    lane = lax.broadcasted_iota(jnp.int32, (GRID_W, LANE), 1)
    qc = lax.broadcasted_iota(jnp.int32, (GRID_W, LANE), 0)
    kc = jnp.bitwise_and(lane, GRID_W - 1)
    dcol = jnp.clip(kc - qc, -(WIN_C - 1), WIN_C - 1) + (WIN_C - 1)
    cstart = jnp.clip(qc - WIN_C // 2, 0, GRID_W - WIN_C)
    col_ok = jnp.logical_and(kc >= cstart, kc < cstart + WIN_C)
    lo = lane < GRID_W
    hits = [dcol == b for b in range(RPB_C)]
    tabs = []
    for a in range(RPB_R):
        acc = jnp.zeros((GRID_W, LANE), F32)
        for b in range(RPB_C):
            acc = jnp.where(hits[b], rpb_ref[((l * RPB_R + a) * RPB_C + b) * NA_H + h], acc)
        tabs.append(jnp.where(col_ok, acc, NEG))
    neg = jnp.full((GRID_W, LANE), NEG, F32)
    for vi, (r0, s0) in enumerate(NA_VARIANTS):
        @pl.when(var == vi)
        def _(r0=r0, s0=s0):
            for qi in range(NA_QROWS):
                qr = r0 + qi
                rs = min(max(qr - WIN_R // 2, 0), GRID_ROWS - WIN_R)
                for m in range(NA_KROWS // 2):
                    kl = s0 + 2 * m
                    left = tabs[kl - qr + WIN_R - 1] if rs <= kl < rs + WIN_R else neg
                    right = tabs[kl + 1 - qr + WIN_R - 1] if rs <= kl + 1 < rs + WIN_R else neg
                    blk = left if left is right else jnp.where(lo, left, right)
                    o_ref[GRID_W * qi:GRID_W * (qi + 1), LANE * m:LANE * (m + 1)] = blk


def _na_bias(rpb):
    return pl.pallas_call(
        _na_bias_kernel,
        grid=(DEPTH, 3, NA_H),
        in_specs=[pl.BlockSpec(memory_space=pltpu.SMEM)],
        out_specs=pl.BlockSpec((None, None, None, NA_TQ, NA_TK), lambda l, v, h: (l, v, h, 0, 0)),
        out_shape=jax.ShapeDtypeStruct((DEPTH, 3, NA_H, NA_TQ, NA_TK), F32),
        compiler_params=_params(("arbitrary", "arbitrary", "arbitrary")),
        name="na_bias_tiles",
    )(rpb.reshape(-1))


def _na_lat_kernel(q_ref, k_ref, v_ref, kc_ref, vc_ref, b_ref, o_ref):
    rb = pl.program_id(2)
    start = pl.multiple_of(jnp.clip(NA_QROWS * rb - WIN_R // 2, 0, GRID_ROWS - NA_KROWS) * GRID_W, 256)
    ks = k_ref[pl.ds(start, NA_TK), :].astype(BF16)
    vs = v_ref[pl.ds(start, NA_TK), :].astype(BF16)
    kc = kc_ref[...].astype(BF16)
    vc = vc_ref[...].astype(BF16)
    o = _pair_attention(q_ref[...], [ks, kc], [vs, vc], [lambda u: b_ref[u], None])
    o_ref[...] = o.astype(BF16)


def _na_latent(l, q, k3, v3, kc, vc, bias):
    def variant(rb):
        return jnp.where(rb == 0, 0, jnp.where(rb == NA_RB - 1, 2, 1))

    qoff = T_CTX // NA_TQ
    return pl.pallas_call(
        _na_lat_kernel,
        grid=(N_LAT_B, NA_H // 2, NA_RB),
        in_specs=[
            pl.BlockSpec((NA_TQ, LANE), lambda b, j, r: (qoff + b * NA_RB + r, j)),
            pl.BlockSpec((None, LAT_L, LANE), lambda b, j, r: (1 + b, 0, j)),
            pl.BlockSpec((None, LAT_L, LANE), lambda b, j, r: (1 + b, 0, j)),
            pl.BlockSpec((None, None, PAST, LANE), lambda b, j, r: (b, l, 0, j)),
            pl.BlockSpec((None, None, PAST, LANE), lambda b, j, r: (b, l, 0, j)),
            pl.BlockSpec((None, None, 2, NA_TQ, NA_TK), lambda b, j, r: (l, variant(r), j, 0, 0)),
        ],
        out_specs=pl.BlockSpec((NA_TQ, LANE), lambda b, j, r: (b * NA_RB + r, j)),
        out_shape=jax.ShapeDtypeStruct((T_LAT, NA_W), BF16),
        compiler_params=_params(("arbitrary", "arbitrary", "arbitrary")),
        name="na_latent",
    )(q, k3, v3, kc, vc, bias)


MLA_TP = 512


def _mla_prep_kernel(*refs, with_q):
    if with_q:
        (qa_ref, ckv_ref, kr_ref, cq_ref, sq_ref, ck_ref, sk_ref,
         wq_ref, wqr_ref, wk_ref, wv_ref, qo_ref, ko_ref, vo_ref) = refs
    else:
        ckv_ref, kr_ref, ck_ref, sk_ref, wk_ref, wv_ref, ko_ref, vo_ref = refs
    ckv = ckv_ref[...].astype(BF16)
    krope = kr_ref[:, 0:LANE] * ck_ref[...] + kr_ref[:, LANE:2 * LANE] * sk_ref[...]
    for h in range(MLA_H):
        ko_ref[h] = (_dot(ckv, wk_ref[h]) + krope).astype(BF16)
    for j in range(MLA_H // 2):
        vo_ref[j] = _dot(ckv, wv_ref[j]).astype(BF16)
    if with_q:
        qa = qa_ref[...]
        scale = (NOPE + ROPE) ** -0.5
        for h in range(MLA_H):
            qh = _dot(qa, wq_ref[h]) * cq_ref[...] + _dot(qa, wqr_ref[h]) * sq_ref[...]
            qo_ref[h] = (qh * scale).astype(BF16)


def _mla_prep_latent(l, qa, ckv, kr, cosq, sinq, cosk, sink, wq, wqr, wk, wv2):
    off = T_CTX // MLA_TP
    nblk = LAT_L // MLA_TP
    rows = lambda w: pl.BlockSpec((MLA_TP, w), lambda b, i: (off + b * nblk + i, 0))
    tab = pl.BlockSpec((MLA_TP, LANE), lambda b, i: (i, 0))
    outh = lambda n: pl.BlockSpec((None, n, MLA_TP, LANE), lambda b, i: (b, 0, i, 0))
    return pl.pallas_call(
        functools.partial(_mla_prep_kernel, with_q=True),
        grid=(N_LAT_B, nblk),
        in_specs=[rows(Q_RANK), rows(KV_RANK), rows(2 * LANE), tab, tab, tab, tab,
                  _layer_spec(l, (MLA_H, Q_RANK, LANE)), _layer_spec(l, (MLA_H, Q_RANK, LANE)),
                  _layer_spec(l, (MLA_H, KV_RANK, LANE)), _layer_spec(l, (MLA_H // 2, KV_RANK, LANE))],
        out_specs=[outh(MLA_H), outh(MLA_H), outh(MLA_H // 2)],
        out_shape=[jax.ShapeDtypeStruct((N_LAT_B, MLA_H, LAT_L, LANE), BF16),
                   jax.ShapeDtypeStruct((N_LAT_B, MLA_H, LAT_L, LANE), BF16),
                   jax.ShapeDtypeStruct((N_LAT_B, MLA_H // 2, LAT_L, LANE), BF16)],
        compiler_params=_params(("arbitrary", "arbitrary")),
        name="mla_prep_latent",
    )(qa, ckv, kr, cosq, sinq, cosk, sink, wq, wqr, wk, wv2)


def _mla_prep_cache(l, ckv, kr, ones, zeros, wk, wv2):
    rows = lambda w: pl.BlockSpec((None, None, PAST, w), lambda b: (b, l, 0, 0))
    tab = pl.BlockSpec((PAST, LANE), lambda b: (0, 0))
    outh = lambda n: pl.BlockSpec((None, n, PAST, LANE), lambda b: (b, 0, 0, 0))
    return pl.pallas_call(
        functools.partial(_mla_prep_kernel, with_q=False),
        grid=(N_LAT_B,),
        in_specs=[rows(KV_RANK), rows(2 * LANE), tab, tab,
                  _layer_spec(l, (MLA_H, KV_RANK, LANE)), _layer_spec(l, (MLA_H // 2, KV_RANK, LANE))],
        out_specs=[outh(MLA_H), outh(MLA_H // 2)],
        out_shape=[jax.ShapeDtypeStruct((N_LAT_B, MLA_H, PAST, LANE), BF16),
                   jax.ShapeDtypeStruct((N_LAT_B, MLA_H // 2, PAST, LANE), BF16)],
        compiler_params=_params(("arbitrary",)),
        name="mla_prep_cache",
    )(ckv, kr, ones, zeros, wk, wv2)


MLA_TQ = 256


def _mla_attn_kernel(q_ref, k_ref, kc_ref, v_ref, vc_ref, o_ref):
    lo = _lane_lo()
    v2 = v_ref[0]
    vc2 = vc_ref[0]
    outs = []
    for u in range(2):
        qh = q_ref[u]
        s = _dot_nt(qh, k_ref[u])
        sc = _dot_nt(qh, kc_ref[u])
        m = jnp.maximum(s.max(axis=-1, keepdims=True), sc.max(axis=-1, keepdims=True))
        p = jnp.exp(s - m)
        pc = jnp.exp(sc - m)
        l = p.sum(axis=-1, keepdims=True) + pc.sum(axis=-1, keepdims=True)
        o = _dot(p.astype(BF16), v2) + _dot(pc.astype(BF16), vc2)
        outs.append(o * (1.0 / l))
    o_ref[...] = jnp.where(lo, outs[0], outs[1]).astype(BF16)


def _mla_attention(qh, kh, kch, vh, vch):
    nq = LAT_L // MLA_TQ
    return pl.pallas_call(
        _mla_attn_kernel,
        grid=(N_LAT_B, MLA_H // 2, nq),
        in_specs=[
            pl.BlockSpec((None, 2, MLA_TQ, LANE), lambda b, j, i: (b, j, i, 0)),
            pl.BlockSpec((None, 2, LAT_L, LANE), lambda b, j, i: (b, j, 0, 0)),
            pl.BlockSpec((None, 2, PAST, LANE), lambda b, j, i: (b, j, 0, 0)),
            pl.BlockSpec((None, 1, LAT_L, LANE), lambda b, j, i: (b, j, 0, 0)),
            pl.BlockSpec((None, 1, PAST, LANE), lambda b, j, i: (b, j, 0, 0)),
        ],
        out_specs=pl.BlockSpec((MLA_TQ, LANE), lambda b, j, i: (b * nq + i, j)),
        out_shape=jax.ShapeDtypeStruct((T_LAT, MLA_W), BF16),
        compiler_params=_params(("arbitrary", "arbitrary", "arbitrary")),
        name="mla_attention",
    )(qh, kh, kch, vh, vch)


N_CTX_TILES = T_CTX // TM


def _merge_kernel(*refs, moe):
    if moe:
        (bu_ref, up_ref, un_ref, attc_ref, na_ref, mla_ref, y_ref, mod_ref, wconv_ref, ggrp_ref, wout_ref, gpost_ref,
         gpre_ref, wr_hi_ref, wr_lo_ref, br_ref, yo_ref, h_ref, route_ref) = refs
    else:
        (bu_ref, up_ref, un_ref, attc_ref, na_ref, mla_ref, y_ref, mod_ref, wconv_ref, ggrp_ref, wout_ref, gpost_ref,
         gpre_ref, yo_ref, h_ref) = refs
    i = pl.program_id(0)
    local = lax.broadcasted_iota(jnp.int32, (TM, 1), 0)
    row = i * TM + local
    seq_mask = jnp.where(row < T_CTX, CTX_L - 1, LAT_L - 1)
    pos = jnp.bitwise_and(row, seq_mask)
    u = bu_ref[:, CONV_W:2 * CONV_W]
    prev = jnp.where(local == 0, up_ref[7:8, :], pltpu.roll(u, 1, axis=0))
    prev = jnp.where(pos == 0, 0.0, prev)
    nxt = jnp.where(local == TM - 1, un_ref[0:1, :], pltpu.roll(u, TM - 1, axis=0))
    nxt = jnp.where(pos == seq_mask, 0.0, nxt)
    wc = wconv_ref[...]
    conv = bu_ref[:, 0:CONV_W] * (wc[0:1] * prev + wc[1:2] * u + wc[2:3] * nxt)
    gg = ggrp_ref[...]
    is_ctx = i < N_CTX_TILES
    att_na = jnp.where(is_ctx, attc_ref[:, 0:NA_W], na_ref[...]).astype(F32)
    att_mla = jnp.where(is_ctx, attc_ref[:, NA_W:], mla_ref[...]).astype(F32)
    mixin = jnp.concatenate([
        _rms(conv, gg[:, 0:CONV_W]).astype(BF16),
        _rms(att_na, gg[:, CONV_W:CONV_W + NA_W]).astype(BF16),
        _rms(att_mla, gg[:, CONV_W + NA_W:]).astype(BF16)], axis=-1)
    mix = _dot(mixin, wout_ref[...])
    mod = mod_ref[...]
    y = y_ref[...] + mod[2:3] * _rms(mix, gpost_ref[...])
    yo_ref[...] = y
    h = _rms(y, gpre_ref[...]) * (1.0 + mod[4:5]) + mod[3:4]
    if not moe:
        h_ref[...] = h.astype(BF16)
        return
    h_ref[...] = h
    hi = h.astype(BF16)
    lo = (h - hi.astype(F32)).astype(BF16)
    logits = _dot(hi, wr_hi_ref[...]) + (_dot(lo, wr_hi_ref[...]) + _dot(hi, wr_lo_ref[...])) + br_ref[...]
    lane = lax.broadcasted_iota(jnp.int32, (TM, LANE), 1).astype(F32)
    logits = jnp.where(lane < N_EXP, logits, NEG)
    m1 = logits.max(axis=-1, keepdims=True)
    i1 = jnp.where(logits == m1, lane, float(LANE)).min(axis=-1, keepdims=True)
    rest = jnp.where(lane == i1, NEG, logits)
    m2 = rest.max(axis=-1, keepdims=True)
    i2 = jnp.where(rest == m2, lane, float(LANE)).min(axis=-1, keepdims=True)
    e2 = jnp.exp(m2 - m1)
    g1 = 1.0 / (1.0 + e2)
    g2 = e2 * g1
    route_ref[...] = jnp.where(lane == 0, i1, jnp.where(lane == 1, i2, jnp.where(lane == 2, g1, jnp.where(lane == 3, g2, 0.0))))


def _merge(l, bu, att_ctx, na_lat, mla_lat, y, mods, w_conv, g_grp, w_out, g_post, g_pre_ff, router=None):
    moe = router is not None
    nb8 = TM // 8
    last8 = T_ALL // 8 - 1
    lat_rows = lambda w: pl.BlockSpec((TM, w), lambda i: (jnp.maximum(i - N_CTX_TILES, 0), 0))
    in_specs = [
        _rows(2 * CONV_W),
        pl.BlockSpec((8, CONV_W), lambda i: (jnp.maximum(i * nb8 - 1, 0), 1)),
        pl.BlockSpec((8, CONV_W), lambda i: (jnp.minimum((i + 1) * nb8, last8), 1)),
        pl.BlockSpec((TM, NA_W + MLA_W), lambda i: (jnp.minimum(i, N_CTX_TILES - 1), 0)),
        lat_rows(NA_W),
        lat_rows(MLA_W),
        _rows(D),
        _mod_spec(l),
        _layer_spec(l, (3, CONV_W)),
        _layer_spec(l, (1, D)),
        _layer_spec(l, (D, D)),
        _layer_spec(l, (1, D)),
        _layer_spec(l, (1, D)),
    ]
    args = [bu, bu, bu, att_ctx, na_lat, mla_lat, y, mods, w_conv, g_grp, w_out, g_post, g_pre_ff]
    out_specs = [_rows(D), _rows(D)]
    out_shape = [jax.ShapeDtypeStruct((T_ALL, D), F32), jax.ShapeDtypeStruct((T_ALL, D), F32 if moe else BF16)]
    if moe:
        li = l // 2
        in_specs += [_layer_spec(li, (D, LANE)), _layer_spec(li, (D, LANE)), _layer_spec(li, (1, LANE))]
        args += list(router)
        out_specs.append(_rows(LANE))
        out_shape.append(jax.ShapeDtypeStruct((T_ALL, LANE), F32))
    return pl.pallas_call(
        functools.partial(_merge_kernel, moe=moe),
        grid=(T_ALL // TM,),
        in_specs=in_specs,
        out_specs=out_specs,
        out_shape=out_shape,
        compiler_params=_params(("arbitrary",)),
        name="merge_outproj_moe" if moe else "merge_outproj",
    )(*args)


FF_CHUNK = D_FF // 2


def _ffn_kernel(h_ref, y_ref, mod_ref, w1_ref, w3_ref, w2_ref, gpost_ref, o_ref):
    h = h_ref[...]
    acc = None
    for c in range(D_FF // FF_CHUNK):
        sl = slice(c * FF_CHUNK, (c + 1) * FF_CHUNK)
        a1 = _dot(h, w1_ref[:, sl])
        a3 = _dot(h, w3_ref[:, sl])
        o = _dot((_silu(a1) * a3).astype(BF16), w2_ref[sl, :])
        acc = o if acc is None else acc + o
    o_ref[...] = y_ref[...] + mod_ref[5:6] * _rms(acc, gpost_ref[...])


def _dense_ffn(l, h, y, mods, w1, w3, w2, g_post):
    li = l // 2
    return pl.pallas_call(
        _ffn_kernel,
        grid=(T_ALL // TM,),
        in_specs=[_rows(D), _rows(D), _mod_spec(l), _layer_spec(li, (D, D_FF)), _layer_spec(li, (D, D_FF)),
                  _layer_spec(li, (D_FF, D)), _layer_spec(l, (1, D))],
        out_specs=_rows(D),
        out_shape=jax.ShapeDtypeStruct((T_ALL, D), F32),
        compiler_params=_params(("arbitrary",)),
        name="dense_ffn",
    )(h, y, mods, w1, w3, w2, g_post)


def _moe_kernel(ue_ref, us_ref, un_ref, asg_ref, h_hbm, w1_ref, w3_ref, w2_ref, z_hbm,
                xbuf, ybuf, wb1, wb3, wb2, sem_g, sem_s):
    u = pl.program_id(0)
    j = pl.program_id(1)
    nu = pl.num_programs(0)
    nj = pl.num_programs(1)
    slot = lax.rem(u, 2)
    n = un_ref[u]

    def gather_copy(unit, sl, r):
        tok = lax.shift_right_logical(asg_ref[us_ref[unit] + r], 1)
        return pltpu.make_async_copy(h_hbm.at[pl.ds(tok, 1), :], xbuf.at[sl, pl.ds(r, 1), :], sem_g.at[sl])

    def scatter_copy(unit, sl, r):
        a = asg_ref[us_ref[unit] + r]
        dst = jnp.bitwise_and(a, 1) * T_ALL + lax.shift_right_logical(a, 1)
        return pltpu.make_async_copy(ybuf.at[sl, pl.ds(r, 1), :], z_hbm.at[pl.ds(dst, 1), :], sem_s.at[sl])

    def for_rows(n_rows, fn):
        full = n_rows // 8

        def body8(g, c):
            for rr in range(8):
                fn(g * 8 + rr)
            return c

        def body1(r, c):
            fn(r)
            return c

        lax.fori_loop(0, full, body8, 0)
        lax.fori_loop(full * 8, n_rows, body1, 0)

    def gather_rows(unit):
        return (un_ref[unit] + MOE_SUB - 1) // MOE_SUB * MOE_SUB

    def scatter_rows(unit):
        return un_ref[unit]

    @pl.when(jnp.logical_and(u == 0, j == 0))
    def _():
        for_rows(gather_rows(0), lambda r: gather_copy(0, 0, r).start())

    @pl.when(j == 0)
    def _():
        for_rows(gather_rows(u), lambda r: gather_copy(u, slot, r).wait())

        @pl.when(u + 1 < nu)
        def _():
            for_rows(gather_rows(u + 1), lambda r: gather_copy(u + 1, 1 - slot, r).start())

    def sub_tile(k, first):
        rows = pl.ds(k * MOE_SUB, MOE_SUB)
        x = xbuf[slot, rows, :].astype(BF16)
        if k == 0:
            w1b, w3b, w2b = w1_ref[...].astype(BF16), w3_ref[...].astype(BF16), w2_ref[...].astype(BF16)
            wb1[...] = w1b
            wb3[...] = w3b
            wb2[...] = w2b
        else:
            w1b, w3b, w2b = wb1[...], wb3[...], wb2[...]
        o = _dot((_silu(_dot(x, w1b)) * _dot(x, w3b)).astype(BF16), w2b)
        if first:
            ybuf[slot, rows, :] = o
        else:
            ybuf[slot, rows, :] += o

    for k in range(MOE_U // MOE_SUB):
        live = n > k * MOE_SUB
        pl.when(jnp.logical_and(live, j == 0))(functools.partial(sub_tile, k, True))
        pl.when(jnp.logical_and(live, j > 0))(functools.partial(sub_tile, k, False))

    @pl.when(j == nj - 1)
    def _():
        @pl.when(u > 0)
        def _():
            for_rows(scatter_rows(u - 1), lambda r: scatter_copy(u - 1, 1 - slot, r).wait())

        for_rows(scatter_rows(u), lambda r: scatter_copy(u, slot, r).start())

        @pl.when(u == nu - 1)
        def _():
            for_rows(scatter_rows(u), lambda r: scatter_copy(u, slot, r).wait())


def _moe_experts(li, unit_expert, unit_start, unit_rows, asg, h, w1, w3, w2):
    nj = D_FFE // MOE_TF

    def jj(u, j, un):
        return jnp.where(un[u] > 0, j, nj - 1)

    return pl.pallas_call(
        _moe_kernel,
        grid_spec=pltpu.PrefetchScalarGridSpec(
            num_scalar_prefetch=4,
            grid=(MOE_UNITS, nj),
            in_specs=[
                pl.BlockSpec(memory_space=pl.ANY),
                pl.BlockSpec((None, None, D, MOE_TF), lambda u, j, ue, us, un, asg: (li, ue[u], 0, jj(u, j, un))),
                pl.BlockSpec((None, None, D, MOE_TF), lambda u, j, ue, us, un, asg: (li, ue[u], 0, jj(u, j, un))),
                pl.BlockSpec((None, None, MOE_TF, D), lambda u, j, ue, us, un, asg: (li, ue[u], jj(u, j, un), 0)),
            ],
            out_specs=pl.BlockSpec(memory_space=pl.ANY),
            scratch_shapes=[
                pltpu.VMEM((2, MOE_U, D), F32), pltpu.VMEM((2, MOE_U, D), F32),
                pltpu.VMEM((D, MOE_TF), BF16), pltpu.VMEM((D, MOE_TF), BF16), pltpu.VMEM((MOE_TF, D), BF16),
                pltpu.SemaphoreType.DMA((2,)), pltpu.SemaphoreType.DMA((2,)),
            ],
        ),
        out_shape=jax.ShapeDtypeStruct((MOE_Z_ROWS, D), F32),
        compiler_params=_params(("arbitrary", "arbitrary"), disable_bounds_checks=True),
        name="moe_experts",
    )(unit_expert, unit_start, unit_rows, asg, h, w1, w3, w2)


def _moe_combine_kernel(z0_ref, z1_ref, route_ref, y_ref, mod_ref, gpost_ref, o_ref):
    r = route_ref[...]
    f = r[:, 2:3] * z0_ref[...] + r[:, 3:4] * z1_ref[...]
    o_ref[...] = y_ref[...] + mod_ref[5:6] * _rms(f, gpost_ref[...])


def _moe_combine(l, z, route, y, mods, g_post):
    return pl.pallas_call(
        _moe_combine_kernel,
        grid=(T_ALL // TM,),
        in_specs=[_rows(D), pl.BlockSpec((TM, D), lambda i: (T_ALL // TM + i, 0)), _rows(LANE), _rows(D),
                  _mod_spec(l), _layer_spec(l, (1, D))],
        out_specs=_rows(D),
        out_shape=jax.ShapeDtypeStruct((T_ALL, D), F32),
        compiler_params=_params(("arbitrary",)),
        name="moe_combine",
    )(z, z, route, y, mods, g_post)


def _moe_plan(route):
    e = route[:, 0:2].astype(jnp.int32).reshape(-1)
    experts = jnp.arange(N_EXP, dtype=jnp.int32)
    onehot = (e[:, None] == experts[None, :]).astype(jnp.int32)
    csum = jnp.cumsum(onehot, axis=0)
    counts = csum[-1]
    starts = jnp.cumsum(counts) - counts
    pos = jnp.sum((starts[None, :] + csum - 1) * onehot, axis=1)
    n_asg = 2 * T_ALL
    asg = (jnp.arange(MOE_ASG_LEN, dtype=jnp.int32) % n_asg).at[pos].set(jnp.arange(n_asg, dtype=jnp.int32))
    units_per = (counts + MOE_U - 1) // MOE_U
    unit_end = jnp.cumsum(units_per)
    total = unit_end[-1]
    uidx = jnp.arange(MOE_UNITS, dtype=jnp.int32)
    ue = jnp.minimum(jnp.sum((uidx[:, None] >= unit_end[None, :]).astype(jnp.int32), axis=1), N_EXP - 1)
    sel = (ue[:, None] == experts[None, :]).astype(jnp.int32)
    pick = lambda v: jnp.sum(sel * v[None, :], axis=1)
    k = uidx - pick(unit_end - units_per)
    valid = uidx < total
    unit_start = jnp.where(valid, pick(starts) + MOE_U * k, 0)
    unit_rows = jnp.where(valid, jnp.clip(pick(counts) - MOE_U * k, 0, MOE_U), 0)
    last_expert = jnp.sum(jnp.where(uidx == total - 1, ue, 0))
    unit_expert = jnp.where(valid, ue, last_expert)
    return unit_expert.astype(jnp.int32), unit_start.astype(jnp.int32), unit_rows.astype(jnp.int32), asg


def _rope_tables():
    half, quarter = ROPE // 2, ROPE // 4
    t = np.arange(LAT_L)
    pos = np.stack([t // GRID_W, t % GRID_W], axis=-1).astype(np.float32)
    inv_freq = np.power(np.float32(ROPE_BASE), -np.arange(quarter, dtype=np.float32) * np.float32(2.0) / np.float32(half))
    ang = pos[:, :, None] * inv_freq.astype(np.float32)
    ang = np.concatenate([ang, ang], axis=-1).reshape(LAT_L, ROPE).astype(np.float32)
    cos, sin = np.cos(ang).astype(np.float32), np.sin(ang).astype(np.float32)
    z64, z32 = np.zeros((LAT_L, NOPE), np.float32), np.zeros((LAT_L, LANE - NOPE - ROPE), np.float32)
    cosq = np.concatenate([np.ones((LAT_L, NOPE), np.float32), cos, z32], axis=1)
    cosk = np.concatenate([z64, cos, z32], axis=1)
    sin_t = np.concatenate([z64, sin, z32], axis=1)
    return jnp.asarray(cosq), jnp.asarray(sin_t), jnp.asarray(cosk), jnp.asarray(sin_t)


def _rot_cols(w):
    k = w.shape[:-1]
    w4 = w.reshape(k + (2, 2, ROPE // 4))
    return jnp.concatenate([-w4[..., 1:2, :], w4[..., 0:1, :]], axis=-2).reshape(k + (ROPE,))


def _prep_weights(w_in, w_qb, w_kvb, w_router):
    zeros = lambda *s: jnp.zeros(s, F32)
    w_kr = w_in[:, :, IN_MAIN:]
    w_in_p = jnp.concatenate([
        w_in[:, :, :IN_MAIN],
        zeros(DEPTH, D, NOPE), w_kr, zeros(DEPTH, D, LANE - NOPE - ROPE),
        zeros(DEPTH, D, NOPE), _rot_cols(w_kr), zeros(DEPTH, D, LANE - NOPE - ROPE)], axis=-1).astype(BF16)
    wq4 = w_qb.reshape(DEPTH, Q_RANK, MLA_H, NOPE + ROPE).transpose(0, 2, 1, 3)
    pad_q = zeros(DEPTH, MLA_H, Q_RANK, LANE - NOPE - ROPE)
    wq = jnp.concatenate([wq4, pad_q], axis=-1).astype(BF16)
    wqr = jnp.concatenate([zeros(DEPTH, MLA_H, Q_RANK, NOPE), _rot_cols(wq4[..., NOPE:]), pad_q], axis=-1).astype(BF16)
    wkv4 = w_kvb.reshape(DEPTH, KV_RANK, MLA_H, NOPE + VD).transpose(0, 2, 1, 3)
    wk = jnp.concatenate([wkv4[..., :NOPE], zeros(DEPTH, MLA_H, KV_RANK, LANE - NOPE)], axis=-1).astype(BF16)
    wv = wkv4[..., NOPE:]
    wv2 = jnp.concatenate([wv[:, 0::2], wv[:, 1::2]], axis=-1).astype(BF16)
    wr = jnp.pad(w_router, ((0, 0), (0, 0), (0, LANE - N_EXP)))
    wr_hi = wr.astype(BF16)
    wr_lo = (wr - wr_hi.astype(F32)).astype(BF16)
    return w_in_p, wq, wqr, wk, wv2, wr_hi, wr_lo


def kernel(x_prompt, x_sample, cache_na_k, cache_na_v, cache_mla_ckv, cache_mla_krope, c, c_ctx,
           w_in, w_conv, rpb, g_qa, w_qb, g_kva, w_kvb, g_grp, w_out, w_mod, b_mod,
           g_pre_mix, g_post_mix, g_pre_ff, g_post_ff, w_ff1, w_ff3, w_ff2,
           w_router, b_router, moe_w1, moe_w3, moe_w2):
    y = jnp.concatenate([x_prompt.reshape(T_CTX, D), x_sample.reshape(T_LAT, D)], axis=0)
    cond8 = jnp.concatenate([c_ctx[None, :], c, jnp.zeros((5, D), F32)], axis=0)
    mods = _modulation(cond8, w_mod, b_mod)[:, :3].reshape(DEPTH, 3, 6, D)

    w_in_p, wq, wqr, wk, wv2, wr_hi, wr_lo = _prep_weights(w_in, w_qb, w_kvb, w_router)
    w_out_b = w_out.astype(BF16)
    ff1, ff3, ff2 = w_ff1.astype(BF16), w_ff3.astype(BF16), w_ff2.astype(BF16)
    b_r = jnp.pad(b_router, ((0, 0), (0, LANE - N_EXP)))[:, None, :]
    vec = lambda a: a[:, None, :]
    g_pre_mix, g_post_mix, g_pre_ff, g_post_ff = vec(g_pre_mix), vec(g_post_mix), vec(g_pre_ff), vec(g_post_ff)
    g_qa, g_kva, g_grp = vec(g_qa), vec(g_kva), vec(g_grp)
    cosq, sinq, cosk, sink = _rope_tables()
    ones_t = jnp.ones((PAST, LANE), F32)
    zeros_t = jnp.zeros((PAST, LANE), F32)
    bias = _na_bias(rpb)
    kc = cache_na_k.reshape(N_LAT_B, DEPTH, PAST, NA_W)
    vc = cache_na_v.reshape(N_LAT_B, DEPTH, PAST, NA_W)
    c_kr = jnp.pad(cache_mla_krope, ((0, 0), (0, 0), (0, 0), (NOPE, 2 * LANE - NOPE - ROPE)))

    st_k, st_v, st_ckv, st_kr = [], [], [], []
    for l in range(DEPTH):
        bu, q, k, v, qa, ckv, kr = _in_projection(l, y, mods, g_pre_mix, w_in_p, g_qa, g_kva)
        st_k.append(k[:T_CTX])
        st_v.append(v[:T_CTX])
        st_ckv.append(ckv[:T_CTX])
        st_kr.append(kr[:T_CTX, NOPE:NOPE + ROPE])

        att_ctx = _ctx_attention(l, q, k, v, qa, ckv, kr, wq, wk, wv2)
        na_lat = _na_latent(l, q, k.reshape(3, GROUP, NA_W), v.reshape(3, GROUP, NA_W), kc, vc, bias)
        qh, kh, vh = _mla_prep_latent(l, qa, ckv, kr, cosq, sinq, cosk, sink, wq, wqr, wk, wv2)
        kch, vch = _mla_prep_cache(l, cache_mla_ckv, c_kr, ones_t, zeros_t, wk, wv2)
        mla_lat = _mla_attention(qh, kh, kch, vh, vch)

        li = l // 2
        if l % 2 == 0:
            y, h = _merge(l, bu, att_ctx, na_lat, mla_lat, y, mods, w_conv, g_grp, w_out_b, g_post_mix, g_pre_ff)
            y = _dense_ffn(l, h, y, mods, ff1, ff3, ff2, g_post_ff)
        else:
            y, h, route = _merge(l, bu, att_ctx, na_lat, mla_lat, y, mods, w_conv, g_grp, w_out_b, g_post_mix, g_pre_ff,
                                 router=(wr_hi, wr_lo, b_r))
            unit_expert, unit_start, unit_rows, asg = _moe_plan(route)
            z = _moe_experts(li, unit_expert, unit_start, unit_rows, asg, h, moe_w1, moe_w3, moe_w2)
            y = _moe_combine(l, z, route, y, mods, g_post_ff)

    y_prompt = y[:T_CTX].reshape(N_CTX_B, CTX_L, D)
    y_sample = y[T_CTX:].reshape(N_LAT_B, LAT_L, D)
    stack = lambda xs, shp: jnp.stack([a.reshape((N_CTX_B, CTX_L) + shp) for a in xs], axis=1)
    return (y_prompt, y_sample, stack(st_k, (NA_H, NA_D)), stack(st_v, (NA_H, NA_D)),
            stack(st_ckv, (KV_RANK,)), stack(st_kr, (ROPE,)))
```

```python
import functools

import numpy as np
import jax
import jax.numpy as jnp
from jax import lax
from jax.experimental import pallas as pl
from jax.experimental.pallas import tpu as pltpu

F32 = jnp.float32
BF16 = jnp.bfloat16

D = 1024
DEPTH = 4
N_CTX_B, CTX_L = 16, 256
N_LAT_B, LAT_L = 2, 4096
GRID_W = 64
GRID_ROWS = LAT_L // GRID_W
GROUP = 4096
T_CTX = N_CTX_B * CTX_L
T_LAT = N_LAT_B * LAT_L
T_ALL = T_CTX + T_LAT
PAST = 256
CONV_W = 256
NA_H, NA_D = 6, 64
NA_W = NA_H * NA_D
WIN_R, WIN_C = 8, 16
RPB_R, RPB_C = 2 * WIN_R - 1, 2 * WIN_C - 1
MLA_H, NOPE, ROPE, VD = 6, 64, 32, 64
MLA_W = MLA_H * VD
Q_RANK, KV_RANK = 256, 128
IN_MAIN = 3 * CONV_W + 3 * NA_W + Q_RANK + KV_RANK
IN_PAD = IN_MAIN + 256
D_FF = 2816
N_EXP = 8
D_FFE = 3584
EPS = 1e-6
NEG = -1e30
ROPE_BASE = 10000.0
LANE = 128
VMEM_LIMIT = 56 * 1024 * 1024

TM = 512
MOE_U = 1024
MOE_SUB = 256
MOE_ROW_UNROLL = 32
MOE_TF = 896
MOE_UNITS = 2 * T_ALL // MOE_U + N_EXP
MOE_ASG_LEN = 2 * T_ALL + MOE_U
MOE_Z_ROWS = 2 * T_ALL


def _rms(x, g):
    return x * lax.rsqrt(jnp.mean(x * x, axis=-1, keepdims=True) + EPS) * g


def _silu(x):
    return x * (1.0 / (1.0 + jnp.exp(-x)))


def _dot(a, b):
    return jnp.dot(a, b, preferred_element_type=F32)


def _dot_nt(a, b):
    return lax.dot_general(a, b, (((1,), (1,)), ((), ())), preferred_element_type=F32)


def _params(sem, vmem=VMEM_LIMIT, **kw):
    return pltpu.CompilerParams(dimension_semantics=sem, vmem_limit_bytes=vmem, **kw)


def _layer_spec(l, shape):
    nd = len(shape)
    return pl.BlockSpec((None,) + tuple(shape), lambda *_: (l,) + (0,) * nd, pipeline_mode=pl.Buffered(1))


def _mod_spec(l):
    return pl.BlockSpec((None, None, 6, D), lambda i: (l, i * TM // GROUP, 0, 0))


def _rows(w):
    return pl.BlockSpec((TM, w), lambda i: (i, 0))


MOD_TN = 1536


def _mod_kernel(cond_ref, w_ref, b_ref, o_ref):
    s = _silu(cond_ref[...]).astype(BF16)
    o_ref[0] = _dot(s, w_ref[0].astype(BF16)) + b_ref[0]


def _modulation(cond8, w_mod, b_mod):
    return pl.pallas_call(
        _mod_kernel,
        grid=(DEPTH, 6 * D // MOD_TN),
        in_specs=[
            pl.BlockSpec((8, D), lambda l, n: (0, 0)),
            pl.BlockSpec((1, D, MOD_TN), lambda l, n: (l, 0, n)),
            pl.BlockSpec((1, 1, MOD_TN), lambda l, n: (l, 0, n)),
        ],
        out_specs=pl.BlockSpec((1, 8, MOD_TN), lambda l, n: (l, 0, n)),
        out_shape=jax.ShapeDtypeStruct((DEPTH, 8, 6 * D), F32),
        compiler_params=_params(("arbitrary", "arbitrary")),
        name="adaln_mod",
    )(cond8, w_mod, b_mod.reshape(DEPTH, 1, 6 * D))


def _inproj_kernel(y_ref, mod_ref, gpre_ref, w_ref, gqa_ref, gkva_ref,
                   bu_ref, q_ref, k_ref, v_ref, qa_ref, ckv_ref, kr_ref):
    mod = mod_ref[...]
    h = _rms(y_ref[...], gpre_ref[...]) * (1.0 + mod[1:2]) + mod[0:1]
    z = _dot(h.astype(BF16), w_ref[...])
    bu_ref[:, 0:CONV_W] = z[:, 0:CONV_W]
    bu_ref[:, CONV_W:2 * CONV_W] = z[:, CONV_W:2 * CONV_W] * z[:, 2 * CONV_W:3 * CONV_W]
    o = 3 * CONV_W
    q_ref[...] = (z[:, o:o + NA_W] * (NA_D ** -0.5)).astype(BF16)
    k_ref[...] = z[:, o + NA_W:o + 2 * NA_W]
    v_ref[...] = z[:, o + 2 * NA_W:o + 3 * NA_W]
    o += 3 * NA_W
    qa_ref[...] = _rms(z[:, o:o + Q_RANK], gqa_ref[...]).astype(BF16)
    o += Q_RANK
    ckv_ref[...] = _rms(z[:, o:o + KV_RANK], gkva_ref[...])
    kr_ref[...] = z[:, IN_MAIN:IN_PAD]


def _in_projection(l, y, mods, g_pre, w_in_p, g_qa, g_kva):
    shp = lambda w, dt: jax.ShapeDtypeStruct((T_ALL, w), dt)
    return pl.pallas_call(
        _inproj_kernel,
        grid=(T_ALL // TM,),
        in_specs=[_rows(D), _mod_spec(l), _layer_spec(l, (1, D)), _layer_spec(l, (D, IN_PAD)),
                  _layer_spec(l, (1, Q_RANK)), _layer_spec(l, (1, KV_RANK))],
        out_specs=[_rows(2 * CONV_W), _rows(NA_W), _rows(NA_W), _rows(NA_W), _rows(Q_RANK), _rows(KV_RANK),
                   _rows(2 * LANE)],
        out_shape=[shp(2 * CONV_W, F32), shp(NA_W, BF16), shp(NA_W, F32), shp(NA_W, F32),
                   shp(Q_RANK, BF16), shp(KV_RANK, F32), shp(2 * LANE, F32)],
        compiler_params=_params(("arbitrary",)),
        name="prenorm_inproj",
    )(y, mods, g_pre, w_in_p, g_qa, g_kva)


def _lane_lo():
    return lax.broadcasted_iota(jnp.int32, (1, LANE), 1) < NA_D


def _pair_attention(q2, keys, vals, biases):
    lo = _lane_lo()
    outs = []
    for u in range(2):
        qm = jnp.where(lo if u == 0 else jnp.logical_not(lo), q2, jnp.zeros_like(q2))
        ss = []
        for kk, bb in zip(keys, biases):
            s = _dot_nt(qm, kk)
            if bb is not None:
                s = s + bb(u)
            ss.append(s)
        m = ss[0].max(axis=-1, keepdims=True)
        for s in ss[1:]:
            m = jnp.maximum(m, s.max(axis=-1, keepdims=True))
        acc = None
        l = None
        for s, vv in zip(ss, vals):
            p = jnp.exp(s - m)
            ls = p.sum(axis=-1, keepdims=True)
            o = _dot(p.astype(BF16), vv)
            acc = o if acc is None else acc + o
            l = ls if l is None else l + ls
        outs.append(acc * (1.0 / l))
    return jnp.where(lo, outs[0], outs[1])


def _ctx_attn_kernel(q_ref, k_ref, v_ref, qa_ref, ckv_ref, kr_ref, wq_ref, wk_ref, wv_ref, o_ref):
    for j in range(NA_H // 2):
        sl = slice(LANE * j, LANE * (j + 1))
        k2 = k_ref[:, sl].astype(BF16)
        v2 = v_ref[:, sl].astype(BF16)
        o_ref[:, sl] = _pair_attention(q_ref[:, sl], [k2], [v2], [None]).astype(BF16)
    qa = qa_ref[...]
    ckv = ckv_ref[...].astype(BF16)
    kr = kr_ref[:, 0:LANE]
    scale = (NOPE + ROPE) ** -0.5
    lo = _lane_lo()
    for j in range(MLA_H // 2):
        v2 = _dot(ckv, wv_ref[j]).astype(BF16)
        outs = []
        for u in range(2):
            h = 2 * j + u
            qh = (_dot(qa, wq_ref[h]) * scale).astype(BF16)
            kh = (_dot(ckv, wk_ref[h]) + kr).astype(BF16)
            s = _dot_nt(qh, kh)
            m = s.max(axis=-1, keepdims=True)
            p = jnp.exp(s - m)
            l = p.sum(axis=-1, keepdims=True)
            outs.append(_dot(p.astype(BF16), v2) * (1.0 / l))
        o_ref[:, NA_W + LANE * j:NA_W + LANE * (j + 1)] = jnp.where(lo, outs[0], outs[1]).astype(BF16)


def _ctx_attention(l, q, k, v, qa, ckv, kr, wq, wk, wv2):
    rows = lambda w: pl.BlockSpec((CTX_L, w), lambda b: (b, 0))
    return pl.pallas_call(
        _ctx_attn_kernel,
        grid=(N_CTX_B,),
        in_specs=[rows(NA_W), rows(NA_W), rows(NA_W), rows(Q_RANK), rows(KV_RANK), rows(2 * LANE),
                  _layer_spec(l, (MLA_H, Q_RANK, LANE)), _layer_spec(l, (MLA_H, KV_RANK, LANE)),
                  _layer_spec(l, (MLA_H // 2, KV_RANK, LANE))],
        out_specs=rows(NA_W + MLA_W),
        out_shape=jax.ShapeDtypeStruct((T_CTX, NA_W + MLA_W), BF16),
        compiler_params=_params(("arbitrary",)),
        name="ctx_attention",
    )(q, k, v, qa, ckv, kr, wq, wk, wv2)


NA_QROWS = 8
NA_KROWS = 16
NA_TQ = NA_QROWS * GRID_W
NA_TK = NA_KROWS * GRID_W
NA_RB = GRID_ROWS // NA_QROWS
NA_VARIANTS = ((0, 0), (NA_QROWS, NA_QROWS - WIN_R // 2), (GRID_ROWS - NA_QROWS, GRID_ROWS - NA_KROWS))


def _na_bias_kernel(rpb_ref, o_ref):
    l = pl.program_id(0)
    var = pl.program_id(1)
    h = pl.program_id(2)
    lane = lax.broadcasted_iota(jnp.int32, (GRID_W, LANE), 1)
    qc = lax.broadcasted_iota(jnp.int32, (GRID_W, LANE), 0)
    kc = jnp.bitwise_and(lane, GRID_W - 1)
    dcol = jnp.clip(kc - qc, -(WIN_C - 1), WIN_C - 1) + (WIN_C - 1)
    cstart = jnp.clip(qc - WIN_C // 2, 0, GRID_W - WIN_C)
    col_ok = jnp.logical_and(kc >= cstart, kc < cstart + WIN_C)
    lo = lane < GRID_W
    hits = [dcol == b for b in range(RPB_C)]
    tabs = []
    for a in range(RPB_R):
        acc = jnp.zeros((GRID_W, LANE), F32)
        for b in range(RPB_C):
            acc = jnp.where(hits[b], rpb_ref[((l * RPB_R + a) * RPB_C + b) * NA_H + h], acc)
        tabs.append(jnp.where(col_ok, acc, NEG))
    neg = jnp.full((GRID_W, LANE), NEG, F32)
    for vi, (r0, s0) in enumerate(NA_VARIANTS):
        @pl.when(var == vi)
        def _(r0=r0, s0=s0):
            for qi in range(NA_QROWS):
                qr = r0 + qi
                rs = min(max(qr - WIN_R // 2, 0), GRID_ROWS - WIN_R)
                for m in range(NA_KROWS // 2):
                    kl = s0 + 2 * m
                    left = tabs[kl - qr + WIN_R - 1] if rs <= kl < rs + WIN_R else neg
                    right = tabs[kl + 1 - qr + WIN_R - 1] if rs <= kl + 1 < rs + WIN_R else neg
                    blk = left if left is right else jnp.where(lo, left, right)
                    o_ref[GRID_W * qi:GRID_W * (qi + 1), LANE * m:LANE * (m + 1)] = blk


def _na_bias(rpb):
    return pl.pallas_call(
        _na_bias_kernel,
        grid=(DEPTH, 3, NA_H),
        in_specs=[pl.BlockSpec(memory_space=pltpu.SMEM)],
        out_specs=pl.BlockSpec((None, None, None, NA_TQ, NA_TK), lambda l, v, h: (l, v, h, 0, 0)),
        out_shape=jax.ShapeDtypeStruct((DEPTH, 3, NA_H, NA_TQ, NA_TK), F32),
        compiler_params=_params(("arbitrary", "arbitrary", "arbitrary")),
        name="na_bias_tiles",
    )(rpb.reshape(-1))


def _na_lat_kernel(q_ref, k_ref, v_ref, kc_ref, vc_ref, b_ref, o_ref):
    rb = pl.program_id(2)
    start = pl.multiple_of(jnp.clip(NA_QROWS * rb - WIN_R // 2, 0, GRID_ROWS - NA_KROWS) * GRID_W, 256)
    ks = k_ref[pl.ds(start, NA_TK), :].astype(BF16)
    vs = v_ref[pl.ds(start, NA_TK), :].astype(BF16)
    kc = kc_ref[...].astype(BF16)
    vc = vc_ref[...].astype(BF16)
    o = _pair_attention(q_ref[...], [ks, kc], [vs, vc], [lambda u: b_ref[u], None])
    o_ref[...] = o.astype(BF16)


def _na_latent(l, q, k3, v3, kc, vc, bias):
    def variant(rb):
        return jnp.where(rb == 0, 0, jnp.where(rb == NA_RB - 1, 2, 1))

    qoff = T_CTX // NA_TQ
    return pl.pallas_call(
        _na_lat_kernel,
        grid=(N_LAT_B, NA_H // 2, NA_RB),
        in_specs=[
            pl.BlockSpec((NA_TQ, LANE), lambda b, j, r: (qoff + b * NA_RB + r, j)),
            pl.BlockSpec((None, LAT_L, LANE), lambda b, j, r: (1 + b, 0, j)),
            pl.BlockSpec((None, LAT_L, LANE), lambda b, j, r: (1 + b, 0, j)),
            pl.BlockSpec((None, None, PAST, LANE), lambda b, j, r: (b, l, 0, j)),
            pl.BlockSpec((None, None, PAST, LANE), lambda b, j, r: (b, l, 0, j)),
            pl.BlockSpec((None, None, 2, NA_TQ, NA_TK), lambda b, j, r: (l, variant(r), j, 0, 0)),
        ],
        out_specs=pl.BlockSpec((NA_TQ, LANE), lambda b, j, r: (b * NA_RB + r, j)),
        out_shape=jax.ShapeDtypeStruct((T_LAT, NA_W), BF16),
        compiler_params=_params(("arbitrary", "arbitrary", "arbitrary")),
        name="na_latent",
    )(q, k3, v3, kc, vc, bias)


MLA_TP = 512


def _mla_prep_kernel(*refs, with_q):
    if with_q:
        (qa_ref, ckv_ref, kr_ref, cq_ref, sq_ref, ck_ref, sk_ref,
         wq_ref, wqr_ref, wk_ref, wv_ref, qo_ref, ko_ref, vo_ref) = refs
    else:
        ckv_ref, kr_ref, ck_ref, sk_ref, wk_ref, wv_ref, ko_ref, vo_ref = refs
    ckv = ckv_ref[...].astype(BF16)
    krope = kr_ref[:, 0:LANE] * ck_ref[...] + kr_ref[:, LANE:2 * LANE] * sk_ref[...]
    for h in range(MLA_H):
        ko_ref[h] = (_dot(ckv, wk_ref[h]) + krope).astype(BF16)
    for j in range(MLA_H // 2):
        vo_ref[j] = _dot(ckv, wv_ref[j]).T.astype(BF16)
    if with_q:
        qa = qa_ref[...]
        scale = (NOPE + ROPE) ** -0.5 * LOG2_E
        for h in range(MLA_H):
            qh = _dot(qa, wq_ref[h]) * cq_ref[...] + _dot(qa, wqr_ref[h]) * sq_ref[...]
            qo_ref[h] = (qh * scale).astype(BF16)


def _mla_prep_latent(l, qa, ckv, kr, cosq, sinq, cosk, sink, wq, wqr, wk, wv2):
    off = T_CTX // MLA_TP
    nblk = LAT_L // MLA_TP
    rows = lambda w: pl.BlockSpec((MLA_TP, w), lambda b, i: (off + b * nblk + i, 0))
    tab = pl.BlockSpec((MLA_TP, LANE), lambda b, i: (i, 0))
    outh = lambda n: pl.BlockSpec((None, n, MLA_TP, LANE), lambda b, i: (b, 0, i, 0))
    return pl.pallas_call(
        functools.partial(_mla_prep_kernel, with_q=True),
        grid=(N_LAT_B, nblk),
        in_specs=[rows(Q_RANK), rows(KV_RANK), rows(2 * LANE), tab, tab, tab, tab,
                  _layer_spec(l, (MLA_H, Q_RANK, LANE)), _layer_spec(l, (MLA_H, Q_RANK, LANE)),
                  _layer_spec(l, (MLA_H, KV_RANK, LANE)), _layer_spec(l, (MLA_H // 2, KV_RANK, LANE))],
        out_specs=[outh(MLA_H), outh(MLA_H),
                   pl.BlockSpec((None, MLA_H // 2, LANE, MLA_TP), lambda b, i: (b, 0, 0, i))],
        out_shape=[jax.ShapeDtypeStruct((N_LAT_B, MLA_H, LAT_L, LANE), BF16),
                   jax.ShapeDtypeStruct((N_LAT_B, MLA_H, LAT_L, LANE), BF16),
                   jax.ShapeDtypeStruct((N_LAT_B, MLA_H // 2, LANE, LAT_L), BF16)],
        compiler_params=_params(("arbitrary", "arbitrary")),
        name="mla_prep_latent",
    )(qa, ckv, kr, cosq, sinq, cosk, sink, wq, wqr, wk, wv2)


def _mla_prep_cache(l, ckv, kr, ones, zeros, wk, wv2):
    rows = lambda w: pl.BlockSpec((None, None, PAST, w), lambda b: (b, l, 0, 0))
    tab = pl.BlockSpec((PAST, LANE), lambda b: (0, 0))
    outh = lambda n: pl.BlockSpec((None, n, PAST, LANE), lambda b: (b, 0, 0, 0))
    return pl.pallas_call(
        functools.partial(_mla_prep_kernel, with_q=False),
        grid=(N_LAT_B,),
        in_specs=[rows(KV_RANK), rows(2 * LANE), tab, tab,
                  _layer_spec(l, (MLA_H, KV_RANK, LANE)), _layer_spec(l, (MLA_H // 2, KV_RANK, LANE))],
        out_specs=[outh(MLA_H), pl.BlockSpec((None, MLA_H // 2, LANE, PAST), lambda b: (b, 0, 0, 0))],
        out_shape=[jax.ShapeDtypeStruct((N_LAT_B, MLA_H, PAST, LANE), BF16),
                   jax.ShapeDtypeStruct((N_LAT_B, MLA_H // 2, LANE, PAST), BF16)],
        compiler_params=_params(("arbitrary",)),
        name="mla_prep_cache",
    )(ckv, kr, ones, zeros, wk, wv2)


MLA_TQ = 256
MLA_KC = 512
LOG2_E = 1.4426950408889634


def _mla_attn_kernel(q_ref, k_ref, kc_ref, vt_ref, vct_ref, o_ref):
    lo = _lane_lo()
    chunks = [(k_ref, vt_ref, c * MLA_KC, MLA_KC) for c in range(LAT_L // MLA_KC)] + [(kc_ref, vct_ref, 0, PAST)]
    nc = len(chunks)
    s = [[None] * nc for _ in range(MLA_H)]
    m = [None] * MLA_H
    l = [None] * MLA_H
    acc = [None] * MLA_H

    def scores(h, c):
        kref, _, start, size = chunks[c]
        s[h][c] = _dot_nt(kref[h, start:start + size, :], q_ref[h])
        cm = s[h][c].max(axis=0, keepdims=True)
        m[h] = cm if m[h] is None else jnp.maximum(m[h], cm)

    def values(h, c):
        _, vref, start, size = chunks[c]
        p = jnp.exp2(s[h][c] - m[h])
        ps = p.sum(axis=0, keepdims=True)
        o = _dot(vref[h // 2, :, start:start + size], p.astype(BF16))
        l[h] = ps if l[h] is None else l[h] + ps
        acc[h] = o if acc[h] is None else acc[h] + o

    for h in range(MLA_H + 1):
        for c in range(nc):
            if h > 0:
                values(h - 1, c)
            if h < MLA_H:
                scores(h, c)
    outs = [(acc[h] * (1.0 / l[h])).T for h in range(MLA_H)]
    for j in range(MLA_H // 2):
        o_ref[:, LANE * j:LANE * (j + 1)] = jnp.where(lo, outs[2 * j], outs[2 * j + 1]).astype(BF16)


def _mla_attention(qh, kh, kch, vh, vch):
    nq = LAT_L // MLA_TQ
    whole = lambda *shape: pl.BlockSpec((None,) + shape, lambda b, i: (b,) + (0,) * len(shape))
    return pl.pallas_call(
        _mla_attn_kernel,
        grid=(N_LAT_B, nq),
        in_specs=[
            pl.BlockSpec((None, MLA_H, MLA_TQ, LANE), lambda b, i: (b, 0, i, 0)),
            whole(MLA_H, LAT_L, LANE), whole(MLA_H, PAST, LANE),
            whole(MLA_H // 2, LANE, LAT_L), whole(MLA_H // 2, LANE, PAST),
        ],
        out_specs=pl.BlockSpec((MLA_TQ, MLA_W), lambda b, i: (b * nq + i, 0)),
        out_shape=jax.ShapeDtypeStruct((T_LAT, MLA_W), BF16),
        compiler_params=_params(("arbitrary", "arbitrary")),
        name="mla_attention",
    )(qh, kh, kch, vh, vch)


N_CTX_TILES = T_CTX // TM


def _merge_kernel(*refs, moe):
    if moe:
        (bu_ref, up_ref, un_ref, attc_ref, na_ref, mla_ref, y_ref, mod_ref, wconv_ref, ggrp_ref, wout_ref, gpost_ref,
         gpre_ref, wr_hi_ref, wr_lo_ref, br_ref, yo_ref, h_ref, route_ref) = refs
    else:
        (bu_ref, up_ref, un_ref, attc_ref, na_ref, mla_ref, y_ref, mod_ref, wconv_ref, ggrp_ref, wout_ref, gpost_ref,
         gpre_ref, yo_ref, h_ref) = refs
    i = pl.program_id(0)
    local = lax.broadcasted_iota(jnp.int32, (TM, 1), 0)
    row = i * TM + local
    seq_mask = jnp.where(row < T_CTX, CTX_L - 1, LAT_L - 1)
    pos = jnp.bitwise_and(row, seq_mask)
    u = bu_ref[:, CONV_W:2 * CONV_W]
    prev = jnp.where(local == 0, up_ref[7:8, :], pltpu.roll(u, 1, axis=0))
    prev = jnp.where(pos == 0, 0.0, prev)
    nxt = jnp.where(local == TM - 1, un_ref[0:1, :], pltpu.roll(u, TM - 1, axis=0))
    nxt = jnp.where(pos == seq_mask, 0.0, nxt)
    wc = wconv_ref[...]
    conv = bu_ref[:, 0:CONV_W] * (wc[0:1] * prev + wc[1:2] * u + wc[2:3] * nxt)
    gg = ggrp_ref[...]
    is_ctx = i < N_CTX_TILES
    att_na = jnp.where(is_ctx, attc_ref[:, 0:NA_W], na_ref[...]).astype(F32)
    att_mla = jnp.where(is_ctx, attc_ref[:, NA_W:], mla_ref[...]).astype(F32)
    mixin = jnp.concatenate([
        _rms(conv, gg[:, 0:CONV_W]).astype(BF16),
        _rms(att_na, gg[:, CONV_W:CONV_W + NA_W]).astype(BF16),
        _rms(att_mla, gg[:, CONV_W + NA_W:]).astype(BF16)], axis=-1)
    mix = _dot(mixin, wout_ref[...])
    mod = mod_ref[...]
    y = y_ref[...] + mod[2:3] * _rms(mix, gpost_ref[...])
    yo_ref[...] = y
    h = _rms(y, gpre_ref[...]) * (1.0 + mod[4:5]) + mod[3:4]
    if not moe:
        h_ref[...] = h.astype(BF16)
        return
    h_ref[...] = h
    hi = h.astype(BF16)
    lo = (h - hi.astype(F32)).astype(BF16)
    logits = _dot(hi, wr_hi_ref[...]) + (_dot(lo, wr_hi_ref[...]) + _dot(hi, wr_lo_ref[...])) + br_ref[...]
    lane = lax.broadcasted_iota(jnp.int32, (TM, LANE), 1).astype(F32)
    logits = jnp.where(lane < N_EXP, logits, NEG)
    m1 = logits.max(axis=-1, keepdims=True)
    i1 = jnp.where(logits == m1, lane, float(LANE)).min(axis=-1, keepdims=True)
    rest = jnp.where(lane == i1, NEG, logits)
    m2 = rest.max(axis=-1, keepdims=True)
    i2 = jnp.where(rest == m2, lane, float(LANE)).min(axis=-1, keepdims=True)
    e2 = jnp.exp(m2 - m1)
    g1 = 1.0 / (1.0 + e2)
    g2 = e2 * g1
    route_ref[...] = jnp.where(lane == 0, i1, jnp.where(lane == 1, i2, jnp.where(lane == 2, g1, jnp.where(lane == 3, g2, 0.0))))


def _merge(l, bu, att_ctx, na_lat, mla_lat, y, mods, w_conv, g_grp, w_out, g_post, g_pre_ff, router=None):
    moe = router is not None
    nb8 = TM // 8
    last8 = T_ALL // 8 - 1
    lat_rows = lambda w: pl.BlockSpec((TM, w), lambda i: (jnp.maximum(i - N_CTX_TILES, 0), 0))
    in_specs = [
        _rows(2 * CONV_W),
        pl.BlockSpec((8, CONV_W), lambda i: (jnp.maximum(i * nb8 - 1, 0), 1)),
        pl.BlockSpec((8, CONV_W), lambda i: (jnp.minimum((i + 1) * nb8, last8), 1)),
        pl.BlockSpec((TM, NA_W + MLA_W), lambda i: (jnp.minimum(i, N_CTX_TILES - 1), 0)),
        lat_rows(NA_W),
        lat_rows(MLA_W),
        _rows(D),
        _mod_spec(l),
        _layer_spec(l, (3, CONV_W)),
        _layer_spec(l, (1, D)),
        _layer_spec(l, (D, D)),
        _layer_spec(l, (1, D)),
        _layer_spec(l, (1, D)),
    ]
    args = [bu, bu, bu, att_ctx, na_lat, mla_lat, y, mods, w_conv, g_grp, w_out, g_post, g_pre_ff]
    out_specs = [_rows(D), _rows(D)]
    out_shape = [jax.ShapeDtypeStruct((T_ALL, D), F32), jax.ShapeDtypeStruct((T_ALL, D), F32 if moe else BF16)]
    if moe:
        li = l // 2
        in_specs += [_layer_spec(li, (D, LANE)), _layer_spec(li, (D, LANE)), _layer_spec(li, (1, LANE))]
        args += list(router)
        out_specs.append(_rows(LANE))
        out_shape.append(jax.ShapeDtypeStruct((T_ALL, LANE), F32))
    return pl.pallas_call(
        functools.partial(_merge_kernel, moe=moe),
        grid=(T_ALL // TM,),
        in_specs=in_specs,
        out_specs=out_specs,
        out_shape=out_shape,
        compiler_params=_params(("arbitrary",)),
        name="merge_outproj_moe" if moe else "merge_outproj",
    )(*args)


FF_CHUNK = D_FF // 2


def _ffn_kernel(h_ref, y_ref, mod_ref, w1_ref, w3_ref, w2_ref, gpost_ref, o_ref):
    h = h_ref[...]
    acc = None
    for c in range(D_FF // FF_CHUNK):
        sl = slice(c * FF_CHUNK, (c + 1) * FF_CHUNK)
        a1 = _dot(h, w1_ref[:, sl])
        a3 = _dot(h, w3_ref[:, sl])
        o = _dot((_silu(a1) * a3).astype(BF16), w2_ref[sl, :])
        acc = o if acc is None else acc + o
    o_ref[...] = y_ref[...] + mod_ref[5:6] * _rms(acc, gpost_ref[...])


def _dense_ffn(l, h, y, mods, w1, w3, w2, g_post):
    li = l // 2
    return pl.pallas_call(
        _ffn_kernel,
        grid=(T_ALL // TM,),
        in_specs=[_rows(D), _rows(D), _mod_spec(l), _layer_spec(li, (D, D_FF)), _layer_spec(li, (D, D_FF)),
                  _layer_spec(li, (D_FF, D)), _layer_spec(l, (1, D))],
        out_specs=_rows(D),
        out_shape=jax.ShapeDtypeStruct((T_ALL, D), F32),
        compiler_params=_params(("arbitrary",)),
        name="dense_ffn",
    )(h, y, mods, w1, w3, w2, g_post)


def _moe_kernel(ue_ref, us_ref, un_ref, tok_ref, dst_ref, h_hbm, w1_ref, w3_ref, w2_ref, z_hbm,
                xbuf, ybuf, wb1, wb3, wb2, sem_g, sem_s):
    u = pl.program_id(0)
    j = pl.program_id(1)
    nu = pl.num_programs(0)
    nj = pl.num_programs(1)
    slot = lax.rem(u, 2)
    n = un_ref[u]

    def gather_copy(sl, r8, rs, tok):
        return pltpu.make_async_copy(h_hbm.at[pl.ds(tok, 1), :], xbuf.at[sl, r8, pl.ds(rs, 1), :], sem_g.at[sl])

    def scatter_copy(sl, r8, rs, dst):
        return pltpu.make_async_copy(ybuf.at[sl, r8, pl.ds(rs, 1), :], z_hbm.at[pl.ds(dst, 1), :], sem_s.at[sl])

    def for_rows(n_rows, fn):
        full = n_rows // MOE_ROW_UNROLL

        def body_many(g, c):
            for rr in range(MOE_ROW_UNROLL):
                fn(g * MOE_ROW_UNROLL + rr, g * (MOE_ROW_UNROLL // 8) + rr // 8, rr % 8)
            return c

        def body_one(r, c):
            fn(r, lax.shift_right_logical(r, 3), jnp.bitwise_and(r, 7))
            return c

        lax.fori_loop(0, full, body_many, 0)
        lax.fori_loop(full * MOE_ROW_UNROLL, n_rows, body_one, 0)

    def gather_rows(unit):
        return (un_ref[unit] + MOE_SUB - 1) // MOE_SUB * MOE_SUB

    def start_gather(unit, sl):
        base = us_ref[unit]
        for_rows(gather_rows(unit), lambda r, r8, rs: gather_copy(sl, r8, rs, tok_ref[base + r]).start())

    def wait_gather(unit, sl):
        for_rows(gather_rows(unit), lambda r, r8, rs: gather_copy(sl, 0, 0, 0).wait())

    def start_scatter(unit, sl):
        base = us_ref[unit]
        for_rows(un_ref[unit], lambda r, r8, rs: scatter_copy(sl, r8, rs, dst_ref[base + r]).start())

    def wait_scatter(unit, sl):
        for_rows(un_ref[unit], lambda r, r8, rs: scatter_copy(sl, 0, 0, 0).wait())

    @pl.when(jnp.logical_and(u == 0, j == 0))
    def _():
        start_gather(0, 0)

    @pl.when(j == 0)
    def _():
        wait_gather(u, slot)

        @pl.when(u + 1 < nu)
        def _():
            start_gather(u + 1, 1 - slot)

    def sub_tile(k, first):
        rows = pl.ds(k * (MOE_SUB // 8), MOE_SUB // 8)
        x = xbuf[slot, rows].reshape(MOE_SUB, D).astype(BF16)
        if k == 0:
            w1b, w3b, w2b = w1_ref[...].astype(BF16), w3_ref[...].astype(BF16), w2_ref[...].astype(BF16)
            wb1[...] = w1b
            wb3[...] = w3b
            wb2[...] = w2b
        else:
            w1b, w3b, w2b = wb1[...], wb3[...], wb2[...]
        o = _dot((_silu(_dot(x, w1b)) * _dot(x, w3b)).astype(BF16), w2b).reshape(MOE_SUB // 8, 8, D)
        if first:
            ybuf[slot, rows] = o
        else:
            ybuf[slot, rows] += o

    for k in range(MOE_U // MOE_SUB):
        live = n > k * MOE_SUB
        pl.when(jnp.logical_and(live, j == 0))(functools.partial(sub_tile, k, True))
        pl.when(jnp.logical_and(live, j > 0))(functools.partial(sub_tile, k, False))

    @pl.when(j == nj - 1)
    def _():
        @pl.when(u > 0)
        def _():
            wait_scatter(u - 1, 1 - slot)

        start_scatter(u, slot)

        @pl.when(u == nu - 1)
        def _():
            wait_scatter(u, slot)


def _moe_experts(li, unit_expert, unit_start, unit_rows, row_token, row_dst, h, w1, w3, w2):
    nj = D_FFE // MOE_TF

    def jj(u, j, un):
        return jnp.where(un[u] > 0, j, nj - 1)

    return pl.pallas_call(
        _moe_kernel,
        grid_spec=pltpu.PrefetchScalarGridSpec(
            num_scalar_prefetch=5,
            grid=(MOE_UNITS, nj),
            in_specs=[
                pl.BlockSpec(memory_space=pl.ANY),
                pl.BlockSpec((None, None, D, MOE_TF), lambda u, j, ue, us, un, *_: (li, ue[u], 0, jj(u, j, un))),
                pl.BlockSpec((None, None, D, MOE_TF), lambda u, j, ue, us, un, *_: (li, ue[u], 0, jj(u, j, un))),
                pl.BlockSpec((None, None, MOE_TF, D), lambda u, j, ue, us, un, *_: (li, ue[u], jj(u, j, un), 0)),
            ],
            out_specs=pl.BlockSpec(memory_space=pl.ANY),
            scratch_shapes=[
                pltpu.VMEM((2, MOE_U // 8, 8, D), F32), pltpu.VMEM((2, MOE_U // 8, 8, D), F32),
                pltpu.VMEM((D, MOE_TF), BF16), pltpu.VMEM((D, MOE_TF), BF16), pltpu.VMEM((MOE_TF, D), BF16),
                pltpu.SemaphoreType.DMA((2,)), pltpu.SemaphoreType.DMA((2,)),
            ],
        ),
        out_shape=jax.ShapeDtypeStruct((MOE_Z_ROWS, D), F32),
        compiler_params=_params(("arbitrary", "arbitrary"), disable_bounds_checks=True),
        name="moe_experts",
    )(unit_expert, unit_start, unit_rows, row_token, row_dst, h, w1, w3, w2)


def _moe_combine_kernel(z0_ref, z1_ref, route_ref, y_ref, mod_ref, gpost_ref, o_ref):
    r = route_ref[...]
    f = r[:, 2:3] * z0_ref[...] + r[:, 3:4] * z1_ref[...]
    o_ref[...] = y_ref[...] + mod_ref[5:6] * _rms(f, gpost_ref[...])


def _moe_combine(l, z, route, y, mods, g_post):
    return pl.pallas_call(
        _moe_combine_kernel,
        grid=(T_ALL // TM,),
        in_specs=[_rows(D), pl.BlockSpec((TM, D), lambda i: (T_ALL // TM + i, 0)), _rows(LANE), _rows(D),
                  _mod_spec(l), _layer_spec(l, (1, D))],
        out_specs=_rows(D),
        out_shape=jax.ShapeDtypeStruct((T_ALL, D), F32),
        compiler_params=_params(("arbitrary",)),
        name="moe_combine",
    )(z, z, route, y, mods, g_post)


def _moe_plan(route):
    e = route[:, 0:2].astype(jnp.int32).reshape(-1)
    experts = jnp.arange(N_EXP, dtype=jnp.int32)
    onehot = (e[:, None] == experts[None, :]).astype(jnp.int32)
    csum = jnp.cumsum(onehot, axis=0)
    counts = csum[-1]
    starts = jnp.cumsum(counts) - counts
    pos = jnp.sum((starts[None, :] + csum - 1) * onehot, axis=1)
    n_asg = 2 * T_ALL
    asg = (jnp.arange(MOE_ASG_LEN, dtype=jnp.int32) % n_asg).at[pos].set(jnp.arange(n_asg, dtype=jnp.int32))
    units_per = (counts + MOE_U - 1) // MOE_U
    unit_end = jnp.cumsum(units_per)
    total = unit_end[-1]
    uidx = jnp.arange(MOE_UNITS, dtype=jnp.int32)
    ue = jnp.minimum(jnp.sum((uidx[:, None] >= unit_end[None, :]).astype(jnp.int32), axis=1), N_EXP - 1)
    sel = (ue[:, None] == experts[None, :]).astype(jnp.int32)
    pick = lambda v: jnp.sum(sel * v[None, :], axis=1)
    k = uidx - pick(unit_end - units_per)
    valid = uidx < total
    unit_start = jnp.where(valid, pick(starts) + MOE_U * k, 0)
    unit_rows = jnp.where(valid, jnp.clip(pick(counts) - MOE_U * k, 0, MOE_U), 0)
    last_expert = jnp.sum(jnp.where(uidx == total - 1, ue, 0))
    unit_expert = jnp.where(valid, ue, last_expert)
    row_token = lax.shift_right_logical(asg, 1)
    row_dst = jnp.bitwise_and(asg, 1) * T_ALL + row_token
    return (unit_expert.astype(jnp.int32), unit_start.astype(jnp.int32), unit_rows.astype(jnp.int32),
            row_token, row_dst)


def _rope_tables():
    half, quarter = ROPE // 2, ROPE // 4
    t = np.arange(LAT_L)
    pos = np.stack([t // GRID_W, t % GRID_W], axis=-1).astype(np.float32)
    inv_freq = np.power(np.float32(ROPE_BASE), -np.arange(quarter, dtype=np.float32) * np.float32(2.0) / np.float32(half))
    ang = pos[:, :, None] * inv_freq.astype(np.float32)
    ang = np.concatenate([ang, ang], axis=-1).reshape(LAT_L, ROPE).astype(np.float32)
    cos, sin = np.cos(ang).astype(np.float32), np.sin(ang).astype(np.float32)
    z64, z32 = np.zeros((LAT_L, NOPE), np.float32), np.zeros((LAT_L, LANE - NOPE - ROPE), np.float32)
    cosq = np.concatenate([np.ones((LAT_L, NOPE), np.float32), cos, z32], axis=1)
    cosk = np.concatenate([z64, cos, z32], axis=1)
    sin_t = np.concatenate([z64, sin, z32], axis=1)
    return jnp.asarray(cosq), jnp.asarray(sin_t), jnp.asarray(cosk), jnp.asarray(sin_t)


def _rot_cols(w):
    k = w.shape[:-1]
    w4 = w.reshape(k + (2, 2, ROPE // 4))
    return jnp.concatenate([-w4[..., 1:2, :], w4[..., 0:1, :]], axis=-2).reshape(k + (ROPE,))


def _prep_weights(w_in, w_qb, w_kvb, w_router):
    zeros = lambda *s: jnp.zeros(s, F32)
    w_kr = w_in[:, :, IN_MAIN:]
    w_in_p = jnp.concatenate([
        w_in[:, :, :IN_MAIN],
        zeros(DEPTH, D, NOPE), w_kr, zeros(DEPTH, D, LANE - NOPE - ROPE),
        zeros(DEPTH, D, NOPE), _rot_cols(w_kr), zeros(DEPTH, D, LANE - NOPE - ROPE)], axis=-1).astype(BF16)
    wq4 = w_qb.reshape(DEPTH, Q_RANK, MLA_H, NOPE + ROPE).transpose(0, 2, 1, 3)
    pad_q = zeros(DEPTH, MLA_H, Q_RANK, LANE - NOPE - ROPE)
    wq = jnp.concatenate([wq4, pad_q], axis=-1).astype(BF16)
    wqr = jnp.concatenate([zeros(DEPTH, MLA_H, Q_RANK, NOPE), _rot_cols(wq4[..., NOPE:]), pad_q], axis=-1).astype(BF16)
    wkv4 = w_kvb.reshape(DEPTH, KV_RANK, MLA_H, NOPE + VD).transpose(0, 2, 1, 3)
    wk = jnp.concatenate([wkv4[..., :NOPE], zeros(DEPTH, MLA_H, KV_RANK, LANE - NOPE)], axis=-1).astype(BF16)
    wv = wkv4[..., NOPE:]
    wv2 = jnp.concatenate([wv[:, 0::2], wv[:, 1::2]], axis=-1).astype(BF16)
    wr = jnp.pad(w_router, ((0, 0), (0, 0), (0, LANE - N_EXP)))
    wr_hi = wr.astype(BF16)
    wr_lo = (wr - wr_hi.astype(F32)).astype(BF16)
    return w_in_p, wq, wqr, wk, wv2, wr_hi, wr_lo


def kernel(x_prompt, x_sample, cache_na_k, cache_na_v, cache_mla_ckv, cache_mla_krope, c, c_ctx,
           w_in, w_conv, rpb, g_qa, w_qb, g_kva, w_kvb, g_grp, w_out, w_mod, b_mod,
           g_pre_mix, g_post_mix, g_pre_ff, g_post_ff, w_ff1, w_ff3, w_ff2,
           w_router, b_router, moe_w1, moe_w3, moe_w2):
    y = jnp.concatenate([x_prompt.reshape(T_CTX, D), x_sample.reshape(T_LAT, D)], axis=0)
    cond8 = jnp.concatenate([c_ctx[None, :], c, jnp.zeros((5, D), F32)], axis=0)
    mods = _modulation(cond8, w_mod, b_mod)[:, :3].reshape(DEPTH, 3, 6, D)

    w_in_p, wq, wqr, wk, wv2, wr_hi, wr_lo = _prep_weights(w_in, w_qb, w_kvb, w_router)
    w_out_b = w_out.astype(BF16)
    ff1, ff3, ff2 = w_ff1.astype(BF16), w_ff3.astype(BF16), w_ff2.astype(BF16)
    b_r = jnp.pad(b_router, ((0, 0), (0, LANE - N_EXP)))[:, None, :]
    vec = lambda a: a[:, None, :]
    g_pre_mix, g_post_mix, g_pre_ff, g_post_ff = vec(g_pre_mix), vec(g_post_mix), vec(g_pre_ff), vec(g_post_ff)
    g_qa, g_kva, g_grp = vec(g_qa), vec(g_kva), vec(g_grp)
    cosq, sinq, cosk, sink = _rope_tables()
    ones_t = jnp.ones((PAST, LANE), F32)
    zeros_t = jnp.zeros((PAST, LANE), F32)
    bias = _na_bias(rpb)
    kc = cache_na_k.reshape(N_LAT_B, DEPTH, PAST, NA_W)
    vc = cache_na_v.reshape(N_LAT_B, DEPTH, PAST, NA_W)
    c_kr = jnp.pad(cache_mla_krope, ((0, 0), (0, 0), (0, 0), (NOPE, 2 * LANE - NOPE - ROPE)))

    st_k, st_v, st_ckv, st_kr = [], [], [], []
    for l in range(DEPTH):
        bu, q, k, v, qa, ckv, kr = _in_projection(l, y, mods, g_pre_mix, w_in_p, g_qa, g_kva)
        st_k.append(k[:T_CTX])
        st_v.append(v[:T_CTX])
        st_ckv.append(ckv[:T_CTX])
        st_kr.append(kr[:T_CTX, NOPE:NOPE + ROPE])

        att_ctx = _ctx_attention(l, q, k, v, qa, ckv, kr, wq, wk, wv2)
        na_lat = _na_latent(l, q, k.reshape(3, GROUP, NA_W), v.reshape(3, GROUP, NA_W), kc, vc, bias)
        qh, kh, vh = _mla_prep_latent(l, qa, ckv, kr, cosq, sinq, cosk, sink, wq, wqr, wk, wv2)
        kch, vch = _mla_prep_cache(l, cache_mla_ckv, c_kr, ones_t, zeros_t, wk, wv2)
        mla_lat = _mla_attention(qh, kh, kch, vh, vch)

        li = l // 2
        if l % 2 == 0:
            y, h = _merge(l, bu, att_ctx, na_lat, mla_lat, y, mods, w_conv, g_grp, w_out_b, g_post_mix, g_pre_ff)
            y = _dense_ffn(l, h, y, mods, ff1, ff3, ff2, g_post_ff)
        else:
            y, h, route = _merge(l, bu, att_ctx, na_lat, mla_lat, y, mods, w_conv, g_grp, w_out_b, g_post_mix, g_pre_ff,
                                 router=(wr_hi, wr_lo, b_r))
            z = _moe_experts(li, *_moe_plan(route), h, moe_w1, moe_w3, moe_w2)
            y = _moe_combine(l, z, route, y, mods, g_post_ff)

    y_prompt = y[:T_CTX].reshape(N_CTX_B, CTX_L, D)
    y_sample = y[T_CTX:].reshape(N_LAT_B, LAT_L, D)
    stack = lambda xs, shp: jnp.stack([a.reshape((N_CTX_B, CTX_L) + shp) for a in xs], axis=1)
    return (y_prompt, y_sample, stack(st_k, (NA_H, NA_D)), stack(st_v, (NA_H, NA_D)),
            stack(st_ckv, (KV_RANK,)), stack(st_kr, (ROPE,)))
```

```python
import functools

import numpy as np
import jax
import jax.numpy as jnp
from jax import lax
from jax.experimental import pallas as pl
from jax.experimental.pallas import tpu as pltpu

F32 = jnp.float32
BF16 = jnp.bfloat16

D = 1024
DEPTH = 4
N_CTX_B, CTX_L = 16, 256
N_LAT_B, LAT_L = 2, 4096
GRID_W = 64
GRID_ROWS = LAT_L // GRID_W
GROUP = 4096
T_CTX = N_CTX_B * CTX_L
T_LAT = N_LAT_B * LAT_L
T_ALL = T_CTX + T_LAT
PAST = 256
CONV_W = 256
NA_H, NA_D = 6, 64
NA_W = NA_H * NA_D
WIN_R, WIN_C = 8, 16
RPB_R, RPB_C = 2 * WIN_R - 1, 2 * WIN_C - 1
MLA_H, NOPE, ROPE, VD = 6, 64, 32, 64
MLA_W = MLA_H * VD
Q_RANK, KV_RANK = 256, 128
IN_MAIN = 3 * CONV_W + 3 * NA_W + Q_RANK + KV_RANK
IN_PAD = IN_MAIN + 256
D_FF = 2816
N_EXP = 8
D_FFE = 3584
EPS = 1e-6
NEG = -1e30
ROPE_BASE = 10000.0
LANE = 128
VMEM_LIMIT = 56 * 1024 * 1024

TM = 512
MOE_U = 1024
MOE_SUB = 256
MOE_ROW_UNROLL = 32
MOE_TF = 896
MOE_UNITS = 2 * T_ALL // MOE_U + N_EXP
MOE_ASG_LEN = 2 * T_ALL + MOE_U
MOE_Z_ROWS = 2 * T_ALL


def _rms(x, g):
    return x * lax.rsqrt(jnp.mean(x * x, axis=-1, keepdims=True) + EPS) * g


def _silu(x):
    return x * (1.0 / (1.0 + jnp.exp(-x)))


def _dot(a, b):
    return jnp.dot(a, b, preferred_element_type=F32)


def _dot_nt(a, b):
    return lax.dot_general(a, b, (((1,), (1,)), ((), ())), preferred_element_type=F32)


def _params(sem, vmem=VMEM_LIMIT, **kw):
    return pltpu.CompilerParams(dimension_semantics=sem, vmem_limit_bytes=vmem, **kw)


def _layer_spec(l, shape):
    nd = len(shape)
    return pl.BlockSpec((None,) + tuple(shape), lambda *_: (l,) + (0,) * nd, pipeline_mode=pl.Buffered(1))


def _mod_spec(l):
    return pl.BlockSpec((None, None, 6, D), lambda i: (l, i * TM // GROUP, 0, 0))


def _rows(w):
    return pl.BlockSpec((TM, w), lambda i: (i, 0))


MOD_TN = 1536


def _mod_kernel(cond_ref, w_ref, b_ref, o_ref):
    s = _silu(cond_ref[...]).astype(BF16)
    o_ref[0] = _dot(s, w_ref[0].astype(BF16)) + b_ref[0]


def _modulation(cond8, w_mod, b_mod):
    return pl.pallas_call(
        _mod_kernel,
        grid=(DEPTH, 6 * D // MOD_TN),
        in_specs=[
            pl.BlockSpec((8, D), lambda l, n: (0, 0)),
            pl.BlockSpec((1, D, MOD_TN), lambda l, n: (l, 0, n)),
            pl.BlockSpec((1, 1, MOD_TN), lambda l, n: (l, 0, n)),
        ],
        out_specs=pl.BlockSpec((1, 8, MOD_TN), lambda l, n: (l, 0, n)),
        out_shape=jax.ShapeDtypeStruct((DEPTH, 8, 6 * D), F32),
        compiler_params=_params(("arbitrary", "arbitrary")),
        name="adaln_mod",
    )(cond8, w_mod, b_mod.reshape(DEPTH, 1, 6 * D))


def _inproj_kernel(y_ref, mod_ref, gpre_ref, w_ref, gqa_ref, gkva_ref,
                   bu_ref, q_ref, k_ref, v_ref, qa_ref, ckv_ref, kr_ref):
    mod = mod_ref[...]
    h = _rms(y_ref[...], gpre_ref[...]) * (1.0 + mod[1:2]) + mod[0:1]
    z = _dot(h.astype(BF16), w_ref[...])
    bu_ref[:, 0:CONV_W] = z[:, 0:CONV_W]
    bu_ref[:, CONV_W:2 * CONV_W] = z[:, CONV_W:2 * CONV_W] * z[:, 2 * CONV_W:3 * CONV_W]
    o = 3 * CONV_W
    q_ref[...] = (z[:, o:o + NA_W] * (NA_D ** -0.5 * LOG2_E)).astype(BF16)
    k_ref[...] = z[:, o + NA_W:o + 2 * NA_W]
    v_ref[...] = z[:, o + 2 * NA_W:o + 3 * NA_W]
    o += 3 * NA_W
    qa_ref[...] = _rms(z[:, o:o + Q_RANK], gqa_ref[...]).astype(BF16)
    o += Q_RANK
    ckv_ref[...] = _rms(z[:, o:o + KV_RANK], gkva_ref[...])
    kr_ref[...] = z[:, IN_MAIN:IN_PAD]


def _in_projection(l, y, mods, g_pre, w_in_p, g_qa, g_kva):
    shp = lambda w, dt: jax.ShapeDtypeStruct((T_ALL, w), dt)
    return pl.pallas_call(
        _inproj_kernel,
        grid=(T_ALL // TM,),
        in_specs=[_rows(D), _mod_spec(l), _layer_spec(l, (1, D)), _layer_spec(l, (D, IN_PAD)),
                  _layer_spec(l, (1, Q_RANK)), _layer_spec(l, (1, KV_RANK))],
        out_specs=[_rows(2 * CONV_W), _rows(NA_W), _rows(NA_W), _rows(NA_W), _rows(Q_RANK), _rows(KV_RANK),
                   _rows(2 * LANE)],
        out_shape=[shp(2 * CONV_W, F32), shp(NA_W, BF16), shp(NA_W, F32), shp(NA_W, F32),
                   shp(Q_RANK, BF16), shp(KV_RANK, F32), shp(2 * LANE, F32)],
        compiler_params=_params(("arbitrary",)),
        name="prenorm_inproj",
    )(y, mods, g_pre, w_in_p, g_qa, g_kva)


def _lane_lo():
    return lax.broadcasted_iota(jnp.int32, (1, LANE), 1) < NA_D


def _pair_attention(q2, keys, vals, biases):
    lo = _lane_lo()
    nk = len(keys)
    qm = [jnp.where(lo, q2, jnp.zeros_like(q2)), jnp.where(lo, jnp.zeros_like(q2), q2)]
    s = [[None] * nk for _ in range(2)]
    m = [None, None]
    l = [None, None]
    acc = [None, None]

    def scores(u, c):
        sc = _dot_nt(qm[u], keys[c])
        if biases[c] is not None:
            sc = sc + biases[c](u)
        s[u][c] = sc
        cm = sc.max(axis=-1, keepdims=True)
        m[u] = cm if m[u] is None else jnp.maximum(m[u], cm)

    def values(u, c):
        p = jnp.exp2(s[u][c] - m[u])
        ls = p.sum(axis=-1, keepdims=True)
        o = _dot(p.astype(BF16), vals[c])
        l[u] = ls if l[u] is None else l[u] + ls
        acc[u] = o if acc[u] is None else acc[u] + o

    for c in range(nk):
        scores(0, c)
    for c in range(nk):
        values(0, c)
        scores(1, c)
    for c in range(nk):
        values(1, c)
    return jnp.where(lo, acc[0] * (1.0 / l[0]), acc[1] * (1.0 / l[1]))


def _ctx_attn_kernel(q_ref, k_ref, v_ref, qa_ref, ckv_ref, kr_ref, wq_ref, wk_ref, wv_ref, o_ref):
    for j in range(NA_H // 2):
        sl = slice(LANE * j, LANE * (j + 1))
        k2 = k_ref[:, sl].astype(BF16)
        v2 = v_ref[:, sl].astype(BF16)
        o_ref[:, sl] = _pair_attention(q_ref[:, sl], [k2], [v2], [None]).astype(BF16)
    qa = qa_ref[...]
    ckv = ckv_ref[...].astype(BF16)
    kr = kr_ref[:, 0:LANE]
    scale = (NOPE + ROPE) ** -0.5 * LOG2_E
    lo = _lane_lo()
    for j in range(MLA_H // 2):
        v2 = _dot(ckv, wv_ref[j]).astype(BF16)
        outs = []
        for u in range(2):
            h = 2 * j + u
            qh = (_dot(qa, wq_ref[h]) * scale).astype(BF16)
            kh = (_dot(ckv, wk_ref[h]) + kr).astype(BF16)
            s = _dot_nt(qh, kh)
            m = s.max(axis=-1, keepdims=True)
            p = jnp.exp2(s - m)
            l = p.sum(axis=-1, keepdims=True)
            outs.append(_dot(p.astype(BF16), v2) * (1.0 / l))
        o_ref[:, NA_W + LANE * j:NA_W + LANE * (j + 1)] = jnp.where(lo, outs[0], outs[1]).astype(BF16)


def _ctx_attention(l, q, k, v, qa, ckv, kr, wq, wk, wv2):
    rows = lambda w: pl.BlockSpec((CTX_L, w), lambda b: (b, 0))
    return pl.pallas_call(
        _ctx_attn_kernel,
        grid=(N_CTX_B,),
        in_specs=[rows(NA_W), rows(NA_W), rows(NA_W), rows(Q_RANK), rows(KV_RANK), rows(2 * LANE),
                  _layer_spec(l, (MLA_H, Q_RANK, LANE)), _layer_spec(l, (MLA_H, KV_RANK, LANE)),
                  _layer_spec(l, (MLA_H // 2, KV_RANK, LANE))],
        out_specs=rows(NA_W + MLA_W),
        out_shape=jax.ShapeDtypeStruct((T_CTX, NA_W + MLA_W), BF16),
        compiler_params=_params(("arbitrary",)),
        name="ctx_attention",
    )(q, k, v, qa, ckv, kr, wq, wk, wv2)


NA_QROWS = 8
NA_KROWS = 16
NA_TQ = NA_QROWS * GRID_W
NA_TK = NA_KROWS * GRID_W
NA_RB = GRID_ROWS // NA_QROWS
NA_VARIANTS = ((0, 0), (NA_QROWS, NA_QROWS - WIN_R // 2), (GRID_ROWS - NA_QROWS, GRID_ROWS - NA_KROWS))


def _na_bias_kernel(rpb_ref, o_ref):
    l = pl.program_id(0)
    h = pl.program_id(1)
    lane = lax.broadcasted_iota(jnp.int32, (GRID_W, LANE), 1)
    qc = lax.broadcasted_iota(jnp.int32, (GRID_W, LANE), 0)
    kc = jnp.bitwise_and(lane, GRID_W - 1)
    dcol = jnp.clip(kc - qc, -(WIN_C - 1), WIN_C - 1) + (WIN_C - 1)
    cstart = jnp.clip(qc - WIN_C // 2, 0, GRID_W - WIN_C)
    col_ok = jnp.logical_and(kc >= cstart, kc < cstart + WIN_C)
    lo = lane < GRID_W
    hits = [dcol == b for b in range(RPB_C)]
    tabs = []
    for a in range(RPB_R):
        acc = jnp.zeros((GRID_W, LANE), F32)
        for b in range(RPB_C):
            acc = jnp.where(hits[b], rpb_ref[((l * RPB_R + a) * RPB_C + b) * NA_H + h], acc)
        tabs.append(jnp.where(col_ok, acc * LOG2_E, NEG))
    neg = jnp.full((GRID_W, LANE), NEG, F32)
    for vi, (r0, s0) in enumerate(NA_VARIANTS):
        for qi in range(NA_QROWS):
            qr = r0 + qi
            rs = min(max(qr - WIN_R // 2, 0), GRID_ROWS - WIN_R)
            for m in range(NA_KROWS // 2):
                kl = s0 + 2 * m
                left = tabs[kl - qr + WIN_R - 1] if rs <= kl < rs + WIN_R else neg
                right = tabs[kl + 1 - qr + WIN_R - 1] if rs <= kl + 1 < rs + WIN_R else neg
                blk = left if left is right else jnp.where(lo, left, right)
                o_ref[vi, GRID_W * qi:GRID_W * (qi + 1), LANE * m:LANE * (m + 1)] = blk


def _na_bias(rpb):
    return pl.pallas_call(
        _na_bias_kernel,
        grid=(DEPTH, NA_H),
        in_specs=[pl.BlockSpec(memory_space=pltpu.SMEM)],
        out_specs=pl.BlockSpec((None, 3, None, NA_TQ, NA_TK), lambda l, h: (l, 0, h, 0, 0)),
        out_shape=jax.ShapeDtypeStruct((DEPTH, 3, NA_H, NA_TQ, NA_TK), F32),
        compiler_params=_params(("arbitrary", "arbitrary")),
        name="na_bias_tiles",
    )(rpb.reshape(-1))


def _na_lat_kernel(q_ref, k_ref, v_ref, kc_ref, vc_ref, b_ref, o_ref):
    rb = pl.program_id(2)
    start = pl.multiple_of(jnp.clip(NA_QROWS * rb - WIN_R // 2, 0, GRID_ROWS - NA_KROWS) * GRID_W, 256)
    half = NA_TK // 2
    keys, vals, biases = [], [], []
    for c in range(2):
        keys.append(k_ref[pl.ds(start + c * half, half), :].astype(BF16))
        vals.append(v_ref[pl.ds(start + c * half, half), :].astype(BF16))
        biases.append(lambda u, c=c: b_ref[u, :, c * half:(c + 1) * half])
    keys.append(kc_ref[...].astype(BF16))
    vals.append(vc_ref[...].astype(BF16))
    biases.append(None)
    o_ref[...] = _pair_attention(q_ref[...], keys, vals, biases).astype(BF16)


def _na_latent(l, q, k3, v3, kc, vc, bias):
    def variant(rb):
        return jnp.where(rb == 0, 0, jnp.where(rb == NA_RB - 1, 2, 1))

    qoff = T_CTX // NA_TQ
    return pl.pallas_call(
        _na_lat_kernel,
        grid=(N_LAT_B, NA_H // 2, NA_RB),
        in_specs=[
            pl.BlockSpec((NA_TQ, LANE), lambda b, j, r: (qoff + b * NA_RB + r, j)),
            pl.BlockSpec((None, LAT_L, LANE), lambda b, j, r: (1 + b, 0, j)),
            pl.BlockSpec((None, LAT_L, LANE), lambda b, j, r: (1 + b, 0, j)),
            pl.BlockSpec((None, None, PAST, LANE), lambda b, j, r: (b, l, 0, j)),
            pl.BlockSpec((None, None, PAST, LANE), lambda b, j, r: (b, l, 0, j)),
            pl.BlockSpec((None, None, 2, NA_TQ, NA_TK), lambda b, j, r: (l, variant(r), j, 0, 0)),
        ],
        out_specs=pl.BlockSpec((NA_TQ, LANE), lambda b, j, r: (b * NA_RB + r, j)),
        out_shape=jax.ShapeDtypeStruct((T_LAT, NA_W), BF16),
        compiler_params=_params(("arbitrary", "arbitrary", "arbitrary")),
        name="na_latent",
    )(q, k3, v3, kc, vc, bias)


MLA_TP = 512


def _mla_prep_kernel(*refs, with_q):
    if with_q:
        (qa_ref, ckv_ref, kr_ref, cq_ref, sq_ref, ck_ref, sk_ref,
         wq_ref, wqr_ref, wk_ref, wv_ref, qo_ref, ko_ref, vo_ref) = refs
    else:
        ckv_ref, kr_ref, ck_ref, sk_ref, wk_ref, wv_ref, ko_ref, vo_ref = refs
    ckv = ckv_ref[...].astype(BF16)
    krope = kr_ref[:, 0:LANE] * ck_ref[...] + kr_ref[:, LANE:2 * LANE] * sk_ref[...]
    for h in range(MLA_H):
        ko_ref[h] = (_dot(ckv, wk_ref[h]) + krope).astype(BF16)
    for j in range(MLA_H // 2):
        vo_ref[j] = _dot(ckv, wv_ref[j]).T.astype(BF16)
    if with_q:
        qa = qa_ref[...]
        scale = (NOPE + ROPE) ** -0.5 * LOG2_E
        for h in range(MLA_H):
            qh = _dot(qa, wq_ref[h]) * cq_ref[...] + _dot(qa, wqr_ref[h]) * sq_ref[...]
            qo_ref[h] = (qh * scale).astype(BF16)


def _mla_prep_latent(l, qa, ckv, kr, cosq, sinq, cosk, sink, wq, wqr, wk, wv2):
    off = T_CTX // MLA_TP
    nblk = LAT_L // MLA_TP
    rows = lambda w: pl.BlockSpec((MLA_TP, w), lambda b, i: (off + b * nblk + i, 0))
    tab = pl.BlockSpec((MLA_TP, LANE), lambda b, i: (i, 0))
    outh = lambda n: pl.BlockSpec((None, n, MLA_TP, LANE), lambda b, i: (b, 0, i, 0))
    return pl.pallas_call(
        functools.partial(_mla_prep_kernel, with_q=True),
        grid=(N_LAT_B, nblk),
        in_specs=[rows(Q_RANK), rows(KV_RANK), rows(2 * LANE), tab, tab, tab, tab,
                  _layer_spec(l, (MLA_H, Q_RANK, LANE)), _layer_spec(l, (MLA_H, Q_RANK, LANE)),
                  _layer_spec(l, (MLA_H, KV_RANK, LANE)), _layer_spec(l, (MLA_H // 2, KV_RANK, LANE))],
        out_specs=[outh(MLA_H), outh(MLA_H),
                   pl.BlockSpec((None, MLA_H // 2, LANE, MLA_TP), lambda b, i: (b, 0, 0, i))],
        out_shape=[jax.ShapeDtypeStruct((N_LAT_B, MLA_H, LAT_L, LANE), BF16),
                   jax.ShapeDtypeStruct((N_LAT_B, MLA_H, LAT_L, LANE), BF16),
                   jax.ShapeDtypeStruct((N_LAT_B, MLA_H // 2, LANE, LAT_L), BF16)],
        compiler_params=_params(("arbitrary", "arbitrary")),
        name="mla_prep_latent",
    )(qa, ckv, kr, cosq, sinq, cosk, sink, wq, wqr, wk, wv2)


def _mla_prep_cache(l, ckv, kr, ones, zeros, wk, wv2):
    rows = lambda w: pl.BlockSpec((None, None, PAST, w), lambda b: (b, l, 0, 0))
    tab = pl.BlockSpec((PAST, LANE), lambda b: (0, 0))
    outh = lambda n: pl.BlockSpec((None, n, PAST, LANE), lambda b: (b, 0, 0, 0))
    return pl.pallas_call(
        functools.partial(_mla_prep_kernel, with_q=False),
        grid=(N_LAT_B,),
        in_specs=[rows(KV_RANK), rows(2 * LANE), tab, tab,
                  _layer_spec(l, (MLA_H, KV_RANK, LANE)), _layer_spec(l, (MLA_H // 2, KV_RANK, LANE))],
        out_specs=[outh(MLA_H), pl.BlockSpec((None, MLA_H // 2, LANE, PAST), lambda b: (b, 0, 0, 0))],
        out_shape=[jax.ShapeDtypeStruct((N_LAT_B, MLA_H, PAST, LANE), BF16),
                   jax.ShapeDtypeStruct((N_LAT_B, MLA_H // 2, LANE, PAST), BF16)],
        compiler_params=_params(("arbitrary",)),
        name="mla_prep_cache",
    )(ckv, kr, ones, zeros, wk, wv2)


MLA_TQ = 256
MLA_KC = 512
LOG2_E = 1.4426950408889634


def _mla_attn_kernel(q_ref, k_ref, kc_ref, vt_ref, vct_ref, o_ref):
    lo = _lane_lo()
    chunks = [(k_ref, vt_ref, c * MLA_KC, MLA_KC) for c in range(LAT_L // MLA_KC)] + [(kc_ref, vct_ref, 0, PAST)]
    nc = len(chunks)
    s = [[None] * nc for _ in range(MLA_H)]
    m = [None] * MLA_H
    l = [None] * MLA_H
    acc = [None] * MLA_H

    def scores(h, c):
        kref, _, start, size = chunks[c]
        s[h][c] = _dot_nt(kref[h, start:start + size, :], q_ref[h])
        cm = s[h][c].max(axis=0, keepdims=True)
        m[h] = cm if m[h] is None else jnp.maximum(m[h], cm)

    def values(h, c):
        _, vref, start, size = chunks[c]
        p = jnp.exp2(s[h][c] - m[h])
        ps = p.sum(axis=0, keepdims=True)
        o = _dot(vref[h // 2, :, start:start + size], p.astype(BF16))
        l[h] = ps if l[h] is None else l[h] + ps
        acc[h] = o if acc[h] is None else acc[h] + o

    for h in range(MLA_H + 1):
        for c in range(nc):
            if h > 0:
                values(h - 1, c)
            if h < MLA_H:
                scores(h, c)
    outs = [(acc[h] * (1.0 / l[h])).T for h in range(MLA_H)]
    for j in range(MLA_H // 2):
        o_ref[:, LANE * j:LANE * (j + 1)] = jnp.where(lo, outs[2 * j], outs[2 * j + 1]).astype(BF16)


def _mla_attention(qh, kh, kch, vh, vch):
    nq = LAT_L // MLA_TQ
    whole = lambda *shape: pl.BlockSpec((None,) + shape, lambda b, i: (b,) + (0,) * len(shape))
    return pl.pallas_call(
        _mla_attn_kernel,
        grid=(N_LAT_B, nq),
        in_specs=[
            pl.BlockSpec((None, MLA_H, MLA_TQ, LANE), lambda b, i: (b, 0, i, 0)),
            whole(MLA_H, LAT_L, LANE), whole(MLA_H, PAST, LANE),
            whole(MLA_H // 2, LANE, LAT_L), whole(MLA_H // 2, LANE, PAST),
        ],
        out_specs=pl.BlockSpec((MLA_TQ, MLA_W), lambda b, i: (b * nq + i, 0)),
        out_shape=jax.ShapeDtypeStruct((T_LAT, MLA_W), BF16),
        compiler_params=_params(("arbitrary", "arbitrary")),
        name="mla_attention",
    )(qh, kh, kch, vh, vch)


N_CTX_TILES = T_CTX // TM


def _merge_kernel(*refs, moe):
    if moe:
        (bu_ref, up_ref, un_ref, attc_ref, na_ref, mla_ref, y_ref, mod_ref, wconv_ref, ggrp_ref, wout_ref, gpost_ref,
         gpre_ref, wr_hi_ref, wr_lo_ref, br_ref, yo_ref, h_ref, route_ref) = refs
    else:
        (bu_ref, up_ref, un_ref, attc_ref, na_ref, mla_ref, y_ref, mod_ref, wconv_ref, ggrp_ref, wout_ref, gpost_ref,
         gpre_ref, yo_ref, h_ref) = refs
    i = pl.program_id(0)
    local = lax.broadcasted_iota(jnp.int32, (TM, 1), 0)
    row = i * TM + local
    seq_mask = jnp.where(row < T_CTX, CTX_L - 1, LAT_L - 1)
    pos = jnp.bitwise_and(row, seq_mask)
    u = bu_ref[:, CONV_W:2 * CONV_W]
    prev = jnp.where(local == 0, up_ref[7:8, :], pltpu.roll(u, 1, axis=0))
    prev = jnp.where(pos == 0, 0.0, prev)
    nxt = jnp.where(local == TM - 1, un_ref[0:1, :], pltpu.roll(u, TM - 1, axis=0))
    nxt = jnp.where(pos == seq_mask, 0.0, nxt)
    wc = wconv_ref[...]
    conv = bu_ref[:, 0:CONV_W] * (wc[0:1] * prev + wc[1:2] * u + wc[2:3] * nxt)
    gg = ggrp_ref[...]
    is_ctx = i < N_CTX_TILES
    att_na = jnp.where(is_ctx, attc_ref[:, 0:NA_W], na_ref[...]).astype(F32)
    att_mla = jnp.where(is_ctx, attc_ref[:, NA_W:], mla_ref[...]).astype(F32)
    mixin = jnp.concatenate([
        _rms(conv, gg[:, 0:CONV_W]).astype(BF16),
        _rms(att_na, gg[:, CONV_W:CONV_W + NA_W]).astype(BF16),
        _rms(att_mla, gg[:, CONV_W + NA_W:]).astype(BF16)], axis=-1)
    mix = _dot(mixin, wout_ref[...])
    mod = mod_ref[...]
    y = y_ref[...] + mod[2:3] * _rms(mix, gpost_ref[...])
    yo_ref[...] = y
    h = _rms(y, gpre_ref[...]) * (1.0 + mod[4:5]) + mod[3:4]
    if not moe:
        h_ref[...] = h.astype(BF16)
        return
    h_ref[...] = h
    hi = h.astype(BF16)
    lo = (h - hi.astype(F32)).astype(BF16)
    logits = _dot(hi, wr_hi_ref[...]) + (_dot(lo, wr_hi_ref[...]) + _dot(hi, wr_lo_ref[...])) + br_ref[...]
    lane = lax.broadcasted_iota(jnp.int32, (TM, LANE), 1).astype(F32)
    logits = jnp.where(lane < N_EXP, logits, NEG)
    m1 = logits.max(axis=-1, keepdims=True)
    i1 = jnp.where(logits == m1, lane, float(LANE)).min(axis=-1, keepdims=True)
    rest = jnp.where(lane == i1, NEG, logits)
    m2 = rest.max(axis=-1, keepdims=True)
    i2 = jnp.where(rest == m2, lane, float(LANE)).min(axis=-1, keepdims=True)
    e2 = jnp.exp(m2 - m1)
    g1 = 1.0 / (1.0 + e2)
    g2 = e2 * g1
    route_ref[...] = jnp.where(lane == 0, i1, jnp.where(lane == 1, i2, jnp.where(lane == 2, g1, jnp.where(lane == 3, g2, 0.0))))


def _merge(l, bu, att_ctx, na_lat, mla_lat, y, mods, w_conv, g_grp, w_out, g_post, g_pre_ff, router=None):
    moe = router is not None
    nb8 = TM // 8
    last8 = T_ALL // 8 - 1
    lat_rows = lambda w: pl.BlockSpec((TM, w), lambda i: (jnp.maximum(i - N_CTX_TILES, 0), 0))
    in_specs = [
        _rows(2 * CONV_W),
        pl.BlockSpec((8, CONV_W), lambda i: (jnp.maximum(i * nb8 - 1, 0), 1)),
        pl.BlockSpec((8, CONV_W), lambda i: (jnp.minimum((i + 1) * nb8, last8), 1)),
        pl.BlockSpec((TM, NA_W + MLA_W), lambda i: (jnp.minimum(i, N_CTX_TILES - 1), 0)),
        lat_rows(NA_W),
        lat_rows(MLA_W),
        _rows(D),
        _mod_spec(l),
        _layer_spec(l, (3, CONV_W)),
        _layer_spec(l, (1, D)),
        _layer_spec(l, (D, D)),
        _layer_spec(l, (1, D)),
        _layer_spec(l, (1, D)),
    ]
    args = [bu, bu, bu, att_ctx, na_lat, mla_lat, y, mods, w_conv, g_grp, w_out, g_post, g_pre_ff]
    out_specs = [_rows(D), _rows(D)]
    out_shape = [jax.ShapeDtypeStruct((T_ALL, D), F32), jax.ShapeDtypeStruct((T_ALL, D), F32 if moe else BF16)]
    if moe:
        li = l // 2
        in_specs += [_layer_spec(li, (D, LANE)), _layer_spec(li, (D, LANE)), _layer_spec(li, (1, LANE))]
        args += list(router)
        out_specs.append(_rows(LANE))
        out_shape.append(jax.ShapeDtypeStruct((T_ALL, LANE), F32))
    return pl.pallas_call(
        functools.partial(_merge_kernel, moe=moe),
        grid=(T_ALL // TM,),
        in_specs=in_specs,
        out_specs=out_specs,
        out_shape=out_shape,
        compiler_params=_params(("arbitrary",)),
        name="merge_outproj_moe" if moe else "merge_outproj",
    )(*args)


FF_CHUNK = D_FF // 2


def _ffn_kernel(h_ref, y_ref, mod_ref, w1_ref, w3_ref, w2_ref, gpost_ref, o_ref):
    h = h_ref[...]
    acc = None
    for c in range(D_FF // FF_CHUNK):
        sl = slice(c * FF_CHUNK, (c + 1) * FF_CHUNK)
        a1 = _dot(h, w1_ref[:, sl])
        a3 = _dot(h, w3_ref[:, sl])
        o = _dot((_silu(a1) * a3).astype(BF16), w2_ref[sl, :])
        acc = o if acc is None else acc + o
    o_ref[...] = y_ref[...] + mod_ref[5:6] * _rms(acc, gpost_ref[...])


def _dense_ffn(l, h, y, mods, w1, w3, w2, g_post):
    li = l // 2
    return pl.pallas_call(
        _ffn_kernel,
        grid=(T_ALL // TM,),
        in_specs=[_rows(D), _rows(D), _mod_spec(l), _layer_spec(li, (D, D_FF)), _layer_spec(li, (D, D_FF)),
                  _layer_spec(li, (D_FF, D)), _layer_spec(l, (1, D))],
        out_specs=_rows(D),
        out_shape=jax.ShapeDtypeStruct((T_ALL, D), F32),
        compiler_params=_params(("arbitrary",)),
        name="dense_ffn",
    )(h, y, mods, w1, w3, w2, g_post)


def _moe_kernel(ue_ref, us_ref, un_ref, pos_ref, h_hbm, w1_ref, w3_ref, w2_ref, z_hbm,
                xbuf, ybuf, wb1, wb3, wb2, tok_ref, dst_ref, sem_g, sem_s):
    u = pl.program_id(0)
    j = pl.program_id(1)
    nu = pl.num_programs(0)
    nj = pl.num_programs(1)
    slot = lax.rem(u, 2)
    n = un_ref[u]

    @pl.when(jnp.logical_and(u == 0, j == 0))
    def _():
        def invert(t4, c):
            for rr in range(8):
                tok = t4 * 4 + rr // 2
                p = pos_ref[t4 * 8 + rr]
                tok_ref[p] = tok
                dst_ref[p] = tok + (rr % 2) * T_ALL
            return c

        def tail(i, c):
            tok_ref[2 * T_ALL + i] = i
            dst_ref[2 * T_ALL + i] = i
            return c

        lax.fori_loop(0, T_ALL // 4, invert, 0)
        lax.fori_loop(0, MOE_U, tail, 0)

    def gather_copy(sl, r8, rs, tok):
        return pltpu.make_async_copy(h_hbm.at[pl.ds(tok, 1), :], xbuf.at[sl, r8, pl.ds(rs, 1), :], sem_g.at[sl])

    def scatter_copy(sl, r8, rs, dst):
        return pltpu.make_async_copy(ybuf.at[sl, r8, pl.ds(rs, 1), :], z_hbm.at[pl.ds(dst, 1), :], sem_s.at[sl])

    def for_rows(n_rows, fn):
        full = n_rows // MOE_ROW_UNROLL

        def body_many(g, c):
            for rr in range(MOE_ROW_UNROLL):
                fn(g * MOE_ROW_UNROLL + rr, g * (MOE_ROW_UNROLL // 8) + rr // 8, rr % 8)
            return c

        def body_one(r, c):
            fn(r, lax.shift_right_logical(r, 3), jnp.bitwise_and(r, 7))
            return c

        lax.fori_loop(0, full, body_many, 0)
        lax.fori_loop(full * MOE_ROW_UNROLL, n_rows, body_one, 0)

    def gather_rows(unit):
        return (un_ref[unit] + MOE_SUB - 1) // MOE_SUB * MOE_SUB

    def start_gather(unit, sl):
        base = us_ref[unit]
        for_rows(gather_rows(unit), lambda r, r8, rs: gather_copy(sl, r8, rs, tok_ref[base + r]).start())

    def wait_gather(unit, sl):
        for_rows(gather_rows(unit), lambda r, r8, rs: gather_copy(sl, 0, 0, 0).wait())

    def start_scatter(unit, sl):
        base = us_ref[unit]
        for_rows(un_ref[unit], lambda r, r8, rs: scatter_copy(sl, r8, rs, dst_ref[base + r]).start())

    def wait_scatter(unit, sl):
        for_rows(un_ref[unit], lambda r, r8, rs: scatter_copy(sl, 0, 0, 0).wait())

    @pl.when(jnp.logical_and(u == 0, j == 0))
    def _():
        start_gather(0, 0)

    @pl.when(j == 0)
    def _():
        wait_gather(u, slot)

        @pl.when(u + 1 < nu)
        def _():
            start_gather(u + 1, 1 - slot)

    def sub_tile(k, first):
        rows = pl.ds(k * (MOE_SUB // 8), MOE_SUB // 8)
        x = xbuf[slot, rows].reshape(MOE_SUB, D).astype(BF16)
        if k == 0:
            w1b, w3b, w2b = w1_ref[...].astype(BF16), w3_ref[...].astype(BF16), w2_ref[...].astype(BF16)
            wb1[...] = w1b
            wb3[...] = w3b
            wb2[...] = w2b
        else:
            w1b, w3b, w2b = wb1[...], wb3[...], wb2[...]
        o = _dot((_silu(_dot(x, w1b)) * _dot(x, w3b)).astype(BF16), w2b).reshape(MOE_SUB // 8, 8, D)
        if first:
            ybuf[slot, rows] = o
        else:
            ybuf[slot, rows] += o

    for k in range(MOE_U // MOE_SUB):
        live = n > k * MOE_SUB
        pl.when(jnp.logical_and(live, j == 0))(functools.partial(sub_tile, k, True))
        pl.when(jnp.logical_and(live, j > 0))(functools.partial(sub_tile, k, False))

    @pl.when(j == nj - 1)
    def _():
        @pl.when(u > 0)
        def _():
            wait_scatter(u - 1, 1 - slot)

        start_scatter(u, slot)

        @pl.when(u == nu - 1)
        def _():
            wait_scatter(u, slot)


def _moe_experts(li, unit_expert, unit_start, unit_rows, pos, h, w1, w3, w2):
    nj = D_FFE // MOE_TF

    def jj(u, j, un):
        return jnp.where(un[u] > 0, j, nj - 1)

    return pl.pallas_call(
        _moe_kernel,
        grid_spec=pltpu.PrefetchScalarGridSpec(
            num_scalar_prefetch=4,
            grid=(MOE_UNITS, nj),
            in_specs=[
                pl.BlockSpec(memory_space=pl.ANY),
                pl.BlockSpec((None, None, D, MOE_TF), lambda u, j, ue, us, un, *_: (li, ue[u], 0, jj(u, j, un))),
                pl.BlockSpec((None, None, D, MOE_TF), lambda u, j, ue, us, un, *_: (li, ue[u], 0, jj(u, j, un))),
                pl.BlockSpec((None, None, MOE_TF, D), lambda u, j, ue, us, un, *_: (li, ue[u], jj(u, j, un), 0)),
            ],
            out_specs=pl.BlockSpec(memory_space=pl.ANY),
            scratch_shapes=[
                pltpu.VMEM((2, MOE_U // 8, 8, D), F32), pltpu.VMEM((2, MOE_U // 8, 8, D), F32),
                pltpu.VMEM((D, MOE_TF), BF16), pltpu.VMEM((D, MOE_TF), BF16), pltpu.VMEM((MOE_TF, D), BF16),
                pltpu.SMEM((MOE_ASG_LEN,), jnp.int32), pltpu.SMEM((MOE_ASG_LEN,), jnp.int32),
                pltpu.SemaphoreType.DMA((2,)), pltpu.SemaphoreType.DMA((2,)),
            ],
        ),
        out_shape=jax.ShapeDtypeStruct((MOE_Z_ROWS, D), F32),
        compiler_params=_params(("arbitrary", "arbitrary"), disable_bounds_checks=True),
        name="moe_experts",
    )(unit_expert, unit_start, unit_rows, pos, h, w1, w3, w2)


def _moe_combine_kernel(z0_ref, z1_ref, route_ref, y_ref, mod_ref, gpost_ref, o_ref):
    r = route_ref[...]
    f = r[:, 2:3] * z0_ref[...] + r[:, 3:4] * z1_ref[...]
    o_ref[...] = y_ref[...] + mod_ref[5:6] * _rms(f, gpost_ref[...])


def _moe_combine(l, z, route, y, mods, g_post):
    return pl.pallas_call(
        _moe_combine_kernel,
        grid=(T_ALL // TM,),
        in_specs=[_rows(D), pl.BlockSpec((TM, D), lambda i: (T_ALL // TM + i, 0)), _rows(LANE), _rows(D),
                  _mod_spec(l), _layer_spec(l, (1, D))],
        out_specs=_rows(D),
        out_shape=jax.ShapeDtypeStruct((T_ALL, D), F32),
        compiler_params=_params(("arbitrary",)),
        name="moe_combine",
    )(z, z, route, y, mods, g_post)


def _moe_plan(route):
    e = route[:, 0:2].astype(jnp.int32).reshape(-1)
    experts = jnp.arange(N_EXP, dtype=jnp.int32)
    onehot = (e[:, None] == experts[None, :]).astype(jnp.int32)
    csum = jnp.cumsum(onehot, axis=0)
    counts = csum[-1]
    starts = jnp.cumsum(counts) - counts
    pos = jnp.sum((starts[None, :] + csum - 1) * onehot, axis=1)
    units_per = (counts + MOE_U - 1) // MOE_U
    unit_end = jnp.cumsum(units_per)
    total = unit_end[-1]
    uidx = jnp.arange(MOE_UNITS, dtype=jnp.int32)
    ue = jnp.minimum(jnp.sum((uidx[:, None] >= unit_end[None, :]).astype(jnp.int32), axis=1), N_EXP - 1)
    sel = (ue[:, None] == experts[None, :]).astype(jnp.int32)
    pick = lambda v: jnp.sum(sel * v[None, :], axis=1)
    k = uidx - pick(unit_end - units_per)
    valid = uidx < total
    unit_start = jnp.where(valid, pick(starts) + MOE_U * k, 0)
    unit_rows = jnp.where(valid, jnp.clip(pick(counts) - MOE_U * k, 0, MOE_U), 0)
    last_expert = jnp.sum(jnp.where(uidx == total - 1, ue, 0))
    unit_expert = jnp.where(valid, ue, last_expert)
    return (unit_expert.astype(jnp.int32), unit_start.astype(jnp.int32), unit_rows.astype(jnp.int32),
            pos.astype(jnp.int32))


def _rope_tables():
    half, quarter = ROPE // 2, ROPE // 4
    t = np.arange(LAT_L)
    pos = np.stack([t // GRID_W, t % GRID_W], axis=-1).astype(np.float32)
    inv_freq = np.power(np.float32(ROPE_BASE), -np.arange(quarter, dtype=np.float32) * np.float32(2.0) / np.float32(half))
    ang = pos[:, :, None] * inv_freq.astype(np.float32)
    ang = np.concatenate([ang, ang], axis=-1).reshape(LAT_L, ROPE).astype(np.float32)
    cos, sin = np.cos(ang).astype(np.float32), np.sin(ang).astype(np.float32)
    z64, z32 = np.zeros((LAT_L, NOPE), np.float32), np.zeros((LAT_L, LANE - NOPE - ROPE), np.float32)
    cosq = np.concatenate([np.ones((LAT_L, NOPE), np.float32), cos, z32], axis=1)
    cosk = np.concatenate([z64, cos, z32], axis=1)
    sin_t = np.concatenate([z64, sin, z32], axis=1)
    return jnp.asarray(cosq), jnp.asarray(sin_t), jnp.asarray(cosk), jnp.asarray(sin_t)


def _rot_cols(w):
    k = w.shape[:-1]
    w4 = w.reshape(k + (2, 2, ROPE // 4))
    return jnp.concatenate([-w4[..., 1:2, :], w4[..., 0:1, :]], axis=-2).reshape(k + (ROPE,))


def _prep_weights(w_in, w_qb, w_kvb, w_router):
    zeros = lambda *s: jnp.zeros(s, F32)
    w_kr = w_in[:, :, IN_MAIN:]
    w_in_p = jnp.concatenate([
        w_in[:, :, :IN_MAIN],
        zeros(DEPTH, D, NOPE), w_kr, zeros(DEPTH, D, LANE - NOPE - ROPE),
        zeros(DEPTH, D, NOPE), _rot_cols(w_kr), zeros(DEPTH, D, LANE - NOPE - ROPE)], axis=-1).astype(BF16)
    wq4 = w_qb.reshape(DEPTH, Q_RANK, MLA_H, NOPE + ROPE).transpose(0, 2, 1, 3)
    pad_q = zeros(DEPTH, MLA_H, Q_RANK, LANE - NOPE - ROPE)
    wq = jnp.concatenate([wq4, pad_q], axis=-1).astype(BF16)
    wqr = jnp.concatenate([zeros(DEPTH, MLA_H, Q_RANK, NOPE), _rot_cols(wq4[..., NOPE:]), pad_q], axis=-1).astype(BF16)
    wkv4 = w_kvb.reshape(DEPTH, KV_RANK, MLA_H, NOPE + VD).transpose(0, 2, 1, 3)
    wk = jnp.concatenate([wkv4[..., :NOPE], zeros(DEPTH, MLA_H, KV_RANK, LANE - NOPE)], axis=-1).astype(BF16)
    wv = wkv4[..., NOPE:]
    wv2 = jnp.concatenate([wv[:, 0::2], wv[:, 1::2]], axis=-1).astype(BF16)
    wr = jnp.pad(w_router, ((0, 0), (0, 0), (0, LANE - N_EXP)))
    wr_hi = wr.astype(BF16)
    wr_lo = (wr - wr_hi.astype(F32)).astype(BF16)
    return w_in_p, wq, wqr, wk, wv2, wr_hi, wr_lo


def kernel(x_prompt, x_sample, cache_na_k, cache_na_v, cache_mla_ckv, cache_mla_krope, c, c_ctx,
           w_in, w_conv, rpb, g_qa, w_qb, g_kva, w_kvb, g_grp, w_out, w_mod, b_mod,
           g_pre_mix, g_post_mix, g_pre_ff, g_post_ff, w_ff1, w_ff3, w_ff2,
           w_router, b_router, moe_w1, moe_w3, moe_w2):
    y = jnp.concatenate([x_prompt.reshape(T_CTX, D), x_sample.reshape(T_LAT, D)], axis=0)
    cond8 = jnp.concatenate([c_ctx[None, :], c, jnp.zeros((5, D), F32)], axis=0)
    mods = _modulation(cond8, w_mod, b_mod)[:, :3].reshape(DEPTH, 3, 6, D)

    w_in_p, wq, wqr, wk, wv2, wr_hi, wr_lo = _prep_weights(w_in, w_qb, w_kvb, w_router)
    w_out_b = w_out.astype(BF16)
    ff1, ff3, ff2 = w_ff1.astype(BF16), w_ff3.astype(BF16), w_ff2.astype(BF16)
    b_r = jnp.pad(b_router, ((0, 0), (0, LANE - N_EXP)))[:, None, :]
    vec = lambda a: a[:, None, :]
    g_pre_mix, g_post_mix, g_pre_ff, g_post_ff = vec(g_pre_mix), vec(g_post_mix), vec(g_pre_ff), vec(g_post_ff)
    g_qa, g_kva, g_grp = vec(g_qa), vec(g_kva), vec(g_grp)
    cosq, sinq, cosk, sink = _rope_tables()
    ones_t = jnp.ones((PAST, LANE), F32)
    zeros_t = jnp.zeros((PAST, LANE), F32)
    bias = _na_bias(rpb)
    kc = cache_na_k.reshape(N_LAT_B, DEPTH, PAST, NA_W)
    vc = cache_na_v.reshape(N_LAT_B, DEPTH, PAST, NA_W)
    c_kr = jnp.pad(cache_mla_krope, ((0, 0), (0, 0), (0, 0), (NOPE, 2 * LANE - NOPE - ROPE)))

    st_k, st_v, st_ckv, st_kr = [], [], [], []
    for l in range(DEPTH):
        bu, q, k, v, qa, ckv, kr = _in_projection(l, y, mods, g_pre_mix, w_in_p, g_qa, g_kva)
        st_k.append(k[:T_CTX])
        st_v.append(v[:T_CTX])
        st_ckv.append(ckv[:T_CTX])
        st_kr.append(kr[:T_CTX, NOPE:NOPE + ROPE])

        att_ctx = _ctx_attention(l, q, k, v, qa, ckv, kr, wq, wk, wv2)
        na_lat = _na_latent(l, q, k.reshape(3, GROUP, NA_W), v.reshape(3, GROUP, NA_W), kc, vc, bias)
        qh, kh, vh = _mla_prep_latent(l, qa, ckv, kr, cosq, sinq, cosk, sink, wq, wqr, wk, wv2)
        kch, vch = _mla_prep_cache(l, cache_mla_ckv, c_kr, ones_t, zeros_t, wk, wv2)
        mla_lat = _mla_attention(qh, kh, kch, vh, vch)

        li = l // 2
        if l % 2 == 0:
            y, h = _merge(l, bu, att_ctx, na_lat, mla_lat, y, mods, w_conv, g_grp, w_out_b, g_post_mix, g_pre_ff)
            y = _dense_ffn(l, h, y, mods, ff1, ff3, ff2, g_post_ff)
        else:
            y, h, route = _merge(l, bu, att_ctx, na_lat, mla_lat, y, mods, w_conv, g_grp, w_out_b, g_post_mix, g_pre_ff,
                                 router=(wr_hi, wr_lo, b_r))
            z = _moe_experts(li, *_moe_plan(route), h, moe_w1, moe_w3, moe_w2)
            y = _moe_combine(l, z, route, y, mods, g_post_ff)

    y_prompt = y[:T_CTX].reshape(N_CTX_B, CTX_L, D)
    y_sample = y[T_CTX:].reshape(N_LAT_B, LAT_L, D)
    stack = lambda xs, shp: jnp.stack([a.reshape((N_CTX_B, CTX_L) + shp) for a in xs], axis=1)
    return (y_prompt, y_sample, stack(st_k, (NA_H, NA_D)), stack(st_v, (NA_H, NA_D)),
            stack(st_ckv, (KV_RANK,)), stack(st_kr, (ROPE,)))
```

```python
import functools

import numpy as np
import jax
import jax.numpy as jnp
from jax import lax
from jax.experimental import pallas as pl
from jax.experimental.pallas import tpu as pltpu

F32 = jnp.float32
BF16 = jnp.bfloat16

D = 1024
DEPTH = 4
N_CTX_B, CTX_L = 16, 256
N_LAT_B, LAT_L = 2, 4096
GRID_W = 64
GRID_ROWS = LAT_L // GRID_W
GROUP = 4096
T_CTX = N_CTX_B * CTX_L
T_LAT = N_LAT_B * LAT_L
T_ALL = T_CTX + T_LAT
PAST = 256
CONV_W = 256
NA_H, NA_D = 6, 64
NA_W = NA_H * NA_D
WIN_R, WIN_C = 8, 16
RPB_R, RPB_C = 2 * WIN_R - 1, 2 * WIN_C - 1
MLA_H, NOPE, ROPE, VD = 6, 64, 32, 64
MLA_W = MLA_H * VD
Q_RANK, KV_RANK = 256, 128
IN_MAIN = 3 * CONV_W + 3 * NA_W + Q_RANK + KV_RANK
IN_PAD = IN_MAIN + 256
D_FF = 2816
N_EXP = 8
D_FFE = 3584
EPS = 1e-6
NEG = -1e30
ROPE_BASE = 10000.0
LANE = 128
VMEM_LIMIT = 56 * 1024 * 1024

TM = 512
MOE_U = 1024
MOE_SUB = 256
MOE_ROW_UNROLL = 32
MOE_TF = 896
MOE_UNITS = 2 * T_ALL // MOE_U + N_EXP
MOE_ASG_LEN = 2 * T_ALL + MOE_U
MOE_Z_ROWS = 2 * T_ALL


def _rms(x, g):
    return x * lax.rsqrt(jnp.mean(x * x, axis=-1, keepdims=True) + EPS) * g


def _silu(x):
    return x * (1.0 / (1.0 + jnp.exp(-x)))


def _dot(a, b):
    return jnp.dot(a, b, preferred_element_type=F32)


def _dot_nt(a, b):
    return lax.dot_general(a, b, (((1,), (1,)), ((), ())), preferred_element_type=F32)


def _params(sem, vmem=VMEM_LIMIT, **kw):
    return pltpu.CompilerParams(dimension_semantics=sem, vmem_limit_bytes=vmem, **kw)


def _layer_spec(l, shape):
    nd = len(shape)
    return pl.BlockSpec((None,) + tuple(shape), lambda *_: (l,) + (0,) * nd, pipeline_mode=pl.Buffered(1))


def _mod_spec(l):
    return pl.BlockSpec((None, None, 6, D), lambda i: (l, i * TM // GROUP, 0, 0))


def _rows(w):
    return pl.BlockSpec((TM, w), lambda i: (i, 0))


MOD_TN = 1536


def _mod_kernel(cond_ref, w_ref, b_ref, o_ref):
    s = _silu(cond_ref[...]).astype(BF16)
    o_ref[0] = _dot(s, w_ref[0].astype(BF16)) + b_ref[0]


def _modulation(cond8, w_mod, b_mod):
    return pl.pallas_call(
        _mod_kernel,
        grid=(DEPTH, 6 * D // MOD_TN),
        in_specs=[
            pl.BlockSpec((8, D), lambda l, n: (0, 0)),
            pl.BlockSpec((1, D, MOD_TN), lambda l, n: (l, 0, n)),
            pl.BlockSpec((1, 1, MOD_TN), lambda l, n: (l, 0, n)),
        ],
        out_specs=pl.BlockSpec((1, 8, MOD_TN), lambda l, n: (l, 0, n)),
        out_shape=jax.ShapeDtypeStruct((DEPTH, 8, 6 * D), F32),
        compiler_params=_params(("arbitrary", "arbitrary")),
        name="adaln_mod",
    )(cond8, w_mod, b_mod.reshape(DEPTH, 1, 6 * D))


def _inproj_kernel(y_ref, mod_ref, gpre_ref, w_ref, gqa_ref, gkva_ref,
                   bu_ref, q_ref, k_ref, v_ref, qa_ref, ckv_ref, kr_ref):
    mod = mod_ref[...]
    h = _rms(y_ref[...], gpre_ref[...]) * (1.0 + mod[1:2]) + mod[0:1]
    z = _dot(h.astype(BF16), w_ref[...])
    bu_ref[:, 0:CONV_W] = z[:, 0:CONV_W]
    bu_ref[:, CONV_W:2 * CONV_W] = z[:, CONV_W:2 * CONV_W] * z[:, 2 * CONV_W:3 * CONV_W]
    o = 3 * CONV_W
    q_ref[...] = (z[:, o:o + NA_W] * (NA_D ** -0.5 * LOG2_E)).astype(BF16)
    k_ref[...] = z[:, o + NA_W:o + 2 * NA_W]
    v_ref[...] = z[:, o + 2 * NA_W:o + 3 * NA_W]
    o += 3 * NA_W
    qa_ref[...] = _rms(z[:, o:o + Q_RANK], gqa_ref[...]).astype(BF16)
    o += Q_RANK
    ckv_ref[...] = _rms(z[:, o:o + KV_RANK], gkva_ref[...])
    kr_ref[...] = z[:, IN_MAIN:IN_PAD]


def _in_projection(l, y, mods, g_pre, w_in_p, g_qa, g_kva):
    shp = lambda w, dt: jax.ShapeDtypeStruct((T_ALL, w), dt)
    return pl.pallas_call(
        _inproj_kernel,
        grid=(T_ALL // TM,),
        in_specs=[_rows(D), _mod_spec(l), _layer_spec(l, (1, D)), _layer_spec(l, (D, IN_PAD)),
                  _layer_spec(l, (1, Q_RANK)), _layer_spec(l, (1, KV_RANK))],
        out_specs=[_rows(2 * CONV_W), _rows(NA_W), _rows(NA_W), _rows(NA_W), _rows(Q_RANK), _rows(KV_RANK),
                   _rows(2 * LANE)],
        out_shape=[shp(2 * CONV_W, F32), shp(NA_W, BF16), shp(NA_W, F32), shp(NA_W, F32),
                   shp(Q_RANK, BF16), shp(KV_RANK, F32), shp(2 * LANE, F32)],
        compiler_params=_params(("arbitrary",)),
        name="prenorm_inproj",
    )(y, mods, g_pre, w_in_p, g_qa, g_kva)


def _lane_lo():
    return lax.broadcasted_iota(jnp.int32, (1, LANE), 1) < NA_D


def _pair_attention(q2, keys, vals, biases):
    lo = _lane_lo()
    nk = len(keys)
    qm = [jnp.where(lo, q2, jnp.zeros_like(q2)), jnp.where(lo, jnp.zeros_like(q2), q2)]
    s = [[None] * nk for _ in range(2)]
    m = [None, None]
    l = [None, None]
    acc = [None, None]

    def scores(u, c):
        sc = _dot_nt(qm[u], keys[c])
        if biases[c] is not None:
            sc = sc + biases[c](u)
        s[u][c] = sc
        cm = sc.max(axis=-1, keepdims=True)
        m[u] = cm if m[u] is None else jnp.maximum(m[u], cm)

    def values(u, c):
        p = jnp.exp2(s[u][c] - m[u])
        ls = p.sum(axis=-1, keepdims=True)
        o = _dot(p.astype(BF16), vals[c])
        l[u] = ls if l[u] is None else l[u] + ls
        acc[u] = o if acc[u] is None else acc[u] + o

    for c in range(nk):
        scores(0, c)
    for c in range(nk):
        values(0, c)
        scores(1, c)
    for c in range(nk):
        values(1, c)
    return jnp.where(lo, acc[0] * (1.0 / l[0]), acc[1] * (1.0 / l[1]))


def _ctx_attn_kernel(q_ref, k_ref, v_ref, qa_ref, ckv_ref, kr_ref, wq_ref, wk_ref, wv_ref, o_ref):
    for j in range(NA_H // 2):
        sl = slice(LANE * j, LANE * (j + 1))
        k2 = k_ref[:, sl].astype(BF16)
        v2 = v_ref[:, sl].astype(BF16)
        o_ref[:, sl] = _pair_attention(q_ref[:, sl], [k2], [v2], [None]).astype(BF16)
    qa = qa_ref[...]
    ckv = ckv_ref[...].astype(BF16)
    kr = kr_ref[:, 0:LANE]
    scale = (NOPE + ROPE) ** -0.5 * LOG2_E
    lo = _lane_lo()
    for j in range(MLA_H // 2):
        v2 = _dot(ckv, wv_ref[j]).astype(BF16)
        outs = []
        for u in range(2):
            h = 2 * j + u
            qh = (_dot(qa, wq_ref[h]) * scale).astype(BF16)
            kh = (_dot(ckv, wk_ref[h]) + kr).astype(BF16)
            s = _dot_nt(qh, kh)
            m = s.max(axis=-1, keepdims=True)
            p = jnp.exp2(s - m)
            l = p.sum(axis=-1, keepdims=True)
            outs.append(_dot(p.astype(BF16), v2) * (1.0 / l))
        o_ref[:, NA_W + LANE * j:NA_W + LANE * (j + 1)] = jnp.where(lo, outs[0], outs[1]).astype(BF16)


def _ctx_attention(l, q, k, v, qa, ckv, kr, wq, wk, wv2):
    rows = lambda w: pl.BlockSpec((CTX_L, w), lambda b: (b, 0))
    return pl.pallas_call(
        _ctx_attn_kernel,
        grid=(N_CTX_B,),
        in_specs=[rows(NA_W), rows(NA_W), rows(NA_W), rows(Q_RANK), rows(KV_RANK), rows(2 * LANE),
                  _layer_spec(l, (MLA_H, Q_RANK, LANE)), _layer_spec(l, (MLA_H, KV_RANK, LANE)),
                  _layer_spec(l, (MLA_H // 2, KV_RANK, LANE))],
        out_specs=rows(NA_W + MLA_W),
        out_shape=jax.ShapeDtypeStruct((T_CTX, NA_W + MLA_W), BF16),
        compiler_params=_params(("arbitrary",)),
        name="ctx_attention",
    )(q, k, v, qa, ckv, kr, wq, wk, wv2)


NA_QROWS = 8
NA_KROWS = 16
NA_TQ = NA_QROWS * GRID_W
NA_TK = NA_KROWS * GRID_W
NA_RB = GRID_ROWS // NA_QROWS
NA_VARIANTS = ((0, 0), (NA_QROWS, NA_QROWS - WIN_R // 2), (GRID_ROWS - NA_QROWS, GRID_ROWS - NA_KROWS))


def _na_bias_kernel(rpb_ref, o_ref):
    l = pl.program_id(0)
    h = pl.program_id(1)
    lane = lax.broadcasted_iota(jnp.int32, (GRID_W, LANE), 1)
    qc = lax.broadcasted_iota(jnp.int32, (GRID_W, LANE), 0)
    kc = jnp.bitwise_and(lane, GRID_W - 1)
    dcol = jnp.clip(kc - qc, -(WIN_C - 1), WIN_C - 1) + (WIN_C - 1)
    cstart = jnp.clip(qc - WIN_C // 2, 0, GRID_W - WIN_C)
    col_ok = jnp.logical_and(kc >= cstart, kc < cstart + WIN_C)
    lo = lane < GRID_W
    hits = [dcol == b for b in range(RPB_C)]
    tabs = []
    for a in range(RPB_R):
        acc = jnp.zeros((GRID_W, LANE), F32)
        for b in range(RPB_C):
            acc = jnp.where(hits[b], rpb_ref[((l * RPB_R + a) * RPB_C + b) * NA_H + h], acc)
        tabs.append(jnp.where(col_ok, acc * LOG2_E, NEG))
    neg = jnp.full((GRID_W, LANE), NEG, F32)
    for vi, (r0, s0) in enumerate(NA_VARIANTS):
        for qi in range(NA_QROWS):
            qr = r0 + qi
            rs = min(max(qr - WIN_R // 2, 0), GRID_ROWS - WIN_R)
            for m in range(NA_KROWS // 2):
                kl = s0 + 2 * m
                left = tabs[kl - qr + WIN_R - 1] if rs <= kl < rs + WIN_R else neg
                right = tabs[kl + 1 - qr + WIN_R - 1] if rs <= kl + 1 < rs + WIN_R else neg
                blk = left if left is right else jnp.where(lo, left, right)
                o_ref[vi, GRID_W * qi:GRID_W * (qi + 1), LANE * m:LANE * (m + 1)] = blk


def _na_bias(rpb):
    return pl.pallas_call(
        _na_bias_kernel,
        grid=(DEPTH, NA_H),
        in_specs=[pl.BlockSpec(memory_space=pltpu.SMEM)],
        out_specs=pl.BlockSpec((None, 3, None, NA_TQ, NA_TK), lambda l, h: (l, 0, h, 0, 0)),
        out_shape=jax.ShapeDtypeStruct((DEPTH, 3, NA_H, NA_TQ, NA_TK), F32),
        compiler_params=_params(("arbitrary", "arbitrary")),
        name="na_bias_tiles",
    )(rpb.reshape(-1))


def _na_lat_kernel(q_ref, k_ref, v_ref, kc_ref, vc_ref, b_ref, o_ref):
    rb = pl.program_id(2)
    start = pl.multiple_of(jnp.clip(NA_QROWS * rb - WIN_R // 2, 0, GRID_ROWS - NA_KROWS) * GRID_W, 256)
    half = NA_TK // 2
    keys, vals, biases = [], [], []
    for c in range(2):
        keys.append(k_ref[pl.ds(start + c * half, half), :].astype(BF16))
        vals.append(v_ref[pl.ds(start + c * half, half), :].astype(BF16))
        biases.append(lambda u, c=c: b_ref[u, :, c * half:(c + 1) * half])
    keys.append(kc_ref[...].astype(BF16))
    vals.append(vc_ref[...].astype(BF16))
    biases.append(None)
    o_ref[...] = _pair_attention(q_ref[...], keys, vals, biases).astype(BF16)


def _na_latent(l, q, k3, v3, kc, vc, bias):
    def variant(rb):
        return jnp.where(rb == 0, 0, jnp.where(rb == NA_RB - 1, 2, 1))

    qoff = T_CTX // NA_TQ
    return pl.pallas_call(
        _na_lat_kernel,
        grid=(N_LAT_B, NA_H // 2, NA_RB),
        in_specs=[
            pl.BlockSpec((NA_TQ, LANE), lambda b, j, r: (qoff + b * NA_RB + r, j)),
            pl.BlockSpec((None, LAT_L, LANE), lambda b, j, r: (1 + b, 0, j)),
            pl.BlockSpec((None, LAT_L, LANE), lambda b, j, r: (1 + b, 0, j)),
            pl.BlockSpec((None, None, PAST, LANE), lambda b, j, r: (b, l, 0, j)),
            pl.BlockSpec((None, None, PAST, LANE), lambda b, j, r: (b, l, 0, j)),
            pl.BlockSpec((None, None, 2, NA_TQ, NA_TK), lambda b, j, r: (l, variant(r), j, 0, 0)),
        ],
        out_specs=pl.BlockSpec((NA_TQ, LANE), lambda b, j, r: (b * NA_RB + r, j)),
        out_shape=jax.ShapeDtypeStruct((T_LAT, NA_W), BF16),
        compiler_params=_params(("arbitrary", "arbitrary", "arbitrary")),
        name="na_latent",
    )(q, k3, v3, kc, vc, bias)


MLA_TP = 512


def _mla_prep_kernel(*refs, with_q):
    if with_q:
        (qa_ref, ckv_ref, kr_ref, cq_ref, sq_ref, ck_ref, sk_ref,
         wq_ref, wqr_ref, wk_ref, wv_ref, qo_ref, ko_ref, vo_ref) = refs
    else:
        ckv_ref, kr_ref, ck_ref, sk_ref, wk_ref, wv_ref, ko_ref, vo_ref = refs
    ckv = ckv_ref[...].astype(BF16)
    krope = kr_ref[:, 0:LANE] * ck_ref[...] + kr_ref[:, LANE:2 * LANE] * sk_ref[...]
    for h in range(MLA_H):
        ko_ref[h] = (_dot(ckv, wk_ref[h]) + krope).astype(BF16)
    for j in range(MLA_H // 2):
        vo_ref[j] = _dot(ckv, wv_ref[j]).T.astype(BF16)
    if with_q:
        qa = qa_ref[...]
        scale = (NOPE + ROPE) ** -0.5 * LOG2_E
        for h in range(MLA_H):
            qh = _dot(qa, wq_ref[h]) * cq_ref[...] + _dot(qa, wqr_ref[h]) * sq_ref[...]
            qo_ref[h] = (qh * scale).astype(BF16)


def _mla_prep_latent(l, qa, ckv, kr, cosq, sinq, cosk, sink, wq, wqr, wk, wv2):
    off = T_CTX // MLA_TP
    nblk = LAT_L // MLA_TP
    rows = lambda w: pl.BlockSpec((MLA_TP, w), lambda b, i: (off + b * nblk + i, 0))
    tab = pl.BlockSpec((MLA_TP, LANE), lambda b, i: (i, 0))
    outh = lambda n: pl.BlockSpec((None, n, MLA_TP, LANE), lambda b, i: (b, 0, i, 0))
    return pl.pallas_call(
        functools.partial(_mla_prep_kernel, with_q=True),
        grid=(N_LAT_B, nblk),
        in_specs=[rows(Q_RANK), rows(KV_RANK), rows(2 * LANE), tab, tab, tab, tab,
                  _layer_spec(l, (MLA_H, Q_RANK, LANE)), _layer_spec(l, (MLA_H, Q_RANK, LANE)),
                  _layer_spec(l, (MLA_H, KV_RANK, LANE)), _layer_spec(l, (MLA_H // 2, KV_RANK, LANE))],
        out_specs=[outh(MLA_H), outh(MLA_H),
                   pl.BlockSpec((None, MLA_H // 2, LANE, MLA_TP), lambda b, i: (b, 0, 0, i))],
        out_shape=[jax.ShapeDtypeStruct((N_LAT_B, MLA_H, LAT_L, LANE), BF16),
                   jax.ShapeDtypeStruct((N_LAT_B, MLA_H, LAT_L, LANE), BF16),
                   jax.ShapeDtypeStruct((N_LAT_B, MLA_H // 2, LANE, LAT_L), BF16)],
        compiler_params=_params(("arbitrary", "arbitrary")),
        name="mla_prep_latent",
    )(qa, ckv, kr, cosq, sinq, cosk, sink, wq, wqr, wk, wv2)


def _mla_prep_cache(l, ckv, kr, ones, zeros, wk, wv2):
    rows = lambda w: pl.BlockSpec((None, None, PAST, w), lambda b: (b, l, 0, 0))
    tab = pl.BlockSpec((PAST, LANE), lambda b: (0, 0))
    outh = lambda n: pl.BlockSpec((None, n, PAST, LANE), lambda b: (b, 0, 0, 0))
    return pl.pallas_call(
        functools.partial(_mla_prep_kernel, with_q=False),
        grid=(N_LAT_B,),
        in_specs=[rows(KV_RANK), rows(2 * LANE), tab, tab,
                  _layer_spec(l, (MLA_H, KV_RANK, LANE)), _layer_spec(l, (MLA_H // 2, KV_RANK, LANE))],
        out_specs=[outh(MLA_H), pl.BlockSpec((None, MLA_H // 2, LANE, PAST), lambda b: (b, 0, 0, 0))],
        out_shape=[jax.ShapeDtypeStruct((N_LAT_B, MLA_H, PAST, LANE), BF16),
                   jax.ShapeDtypeStruct((N_LAT_B, MLA_H // 2, LANE, PAST), BF16)],
        compiler_params=_params(("arbitrary",)),
        name="mla_prep_cache",
    )(ckv, kr, ones, zeros, wk, wv2)


MLA_TQ = 256
MLA_KC = 256
LOG2_E = 1.4426950408889634


def _mla_attn_kernel(q_ref, k_ref, kc_ref, vt_ref, vct_ref, o_ref):
    lo = _lane_lo()
    chunks = [(k_ref, vt_ref, c * MLA_KC, MLA_KC) for c in range(LAT_L // MLA_KC)] + [(kc_ref, vct_ref, 0, PAST)]
    nc = len(chunks)
    p = [[None] * nc for _ in range(MLA_H)]
    ps = [[None] * nc for _ in range(MLA_H)]
    mc = [[None] * nc for _ in range(MLA_H)]
    l = [None] * MLA_H
    acc = [None] * MLA_H

    def scores(h, c):
        kref, _, start, size = chunks[c]
        s = _dot_nt(kref[h, start:start + size, :], q_ref[h])
        cm = s.max(axis=0, keepdims=True)
        mc[h][c] = cm if c == 0 else jnp.maximum(mc[h][c - 1], cm)
        e = jnp.exp2(s - mc[h][c])
        ps[h][c] = e.sum(axis=0, keepdims=True)
        p[h][c] = e.astype(BF16)

    def values(h, c):
        _, vref, start, size = chunks[c]
        a = jnp.exp2(mc[h][c] - mc[h][nc - 1])
        o = _dot(vref[h // 2, :, start:start + size], p[h][c]) * a
        ls = ps[h][c] * a
        l[h] = ls if l[h] is None else l[h] + ls
        acc[h] = o if acc[h] is None else acc[h] + o

    for h in range(MLA_H + 1):
        for c in range(nc):
            if h > 0:
                values(h - 1, c)
            if h < MLA_H:
                scores(h, c)
    outs = [(acc[h] * (1.0 / l[h])).T for h in range(MLA_H)]
    for j in range(MLA_H // 2):
        o_ref[:, LANE * j:LANE * (j + 1)] = jnp.where(lo, outs[2 * j], outs[2 * j + 1]).astype(BF16)


def _mla_attention(qh, kh, kch, vh, vch):
    nq = LAT_L // MLA_TQ
    whole = lambda *shape: pl.BlockSpec((None,) + shape, lambda b, i: (b,) + (0,) * len(shape))
    return pl.pallas_call(
        _mla_attn_kernel,
        grid=(N_LAT_B, nq),
        in_specs=[
            pl.BlockSpec((None, MLA_H, MLA_TQ, LANE), lambda b, i: (b, 0, i, 0)),
            whole(MLA_H, LAT_L, LANE), whole(MLA_H, PAST, LANE),
            whole(MLA_H // 2, LANE, LAT_L), whole(MLA_H // 2, LANE, PAST),
        ],
        out_specs=pl.BlockSpec((MLA_TQ, MLA_W), lambda b, i: (b * nq + i, 0)),
        out_shape=jax.ShapeDtypeStruct((T_LAT, MLA_W), BF16),
        compiler_params=_params(("arbitrary", "arbitrary")),
        name="mla_attention",
    )(qh, kh, kch, vh, vch)


N_CTX_TILES = T_CTX // TM


def _merge_kernel(*refs, moe):
    if moe:
        (bu_ref, up_ref, un_ref, attc_ref, na_ref, mla_ref, y_ref, mod_ref, wconv_ref, ggrp_ref, wout_ref, gpost_ref,
         gpre_ref, wr_hi_ref, wr_lo_ref, br_ref, yo_ref, h_ref, route_ref) = refs
    else:
        (bu_ref, up_ref, un_ref, attc_ref, na_ref, mla_ref, y_ref, mod_ref, wconv_ref, ggrp_ref, wout_ref, gpost_ref,
         gpre_ref, yo_ref, h_ref) = refs
    i = pl.program_id(0)
    local = lax.broadcasted_iota(jnp.int32, (TM, 1), 0)
    row = i * TM + local
    seq_mask = jnp.where(row < T_CTX, CTX_L - 1, LAT_L - 1)
    pos = jnp.bitwise_and(row, seq_mask)
    u = bu_ref[:, CONV_W:2 * CONV_W]
    prev = jnp.where(local == 0, up_ref[7:8, :], pltpu.roll(u, 1, axis=0))
    prev = jnp.where(pos == 0, 0.0, prev)
    nxt = jnp.where(local == TM - 1, un_ref[0:1, :], pltpu.roll(u, TM - 1, axis=0))
    nxt = jnp.where(pos == seq_mask, 0.0, nxt)
    wc = wconv_ref[...]
    conv = bu_ref[:, 0:CONV_W] * (wc[0:1] * prev + wc[1:2] * u + wc[2:3] * nxt)
    gg = ggrp_ref[...]
    is_ctx = i < N_CTX_TILES
    att_na = jnp.where(is_ctx, attc_ref[:, 0:NA_W], na_ref[...]).astype(F32)
    att_mla = jnp.where(is_ctx, attc_ref[:, NA_W:], mla_ref[...]).astype(F32)
    mixin = jnp.concatenate([
        _rms(conv, gg[:, 0:CONV_W]).astype(BF16),
        _rms(att_na, gg[:, CONV_W:CONV_W + NA_W]).astype(BF16),
        _rms(att_mla, gg[:, CONV_W + NA_W:]).astype(BF16)], axis=-1)
    mix = _dot(mixin, wout_ref[...])
    mod = mod_ref[...]
    y = y_ref[...] + mod[2:3] * _rms(mix, gpost_ref[...])
    yo_ref[...] = y
    h = _rms(y, gpre_ref[...]) * (1.0 + mod[4:5]) + mod[3:4]
    if not moe:
        h_ref[...] = h.astype(BF16)
        return
    h_ref[...] = h
    hi = h.astype(BF16)
    lo = (h - hi.astype(F32)).astype(BF16)
    logits = _dot(hi, wr_hi_ref[...]) + (_dot(lo, wr_hi_ref[...]) + _dot(hi, wr_lo_ref[...])) + br_ref[...]
    lane = lax.broadcasted_iota(jnp.int32, (TM, LANE), 1).astype(F32)
    logits = jnp.where(lane < N_EXP, logits, NEG)
    m1 = logits.max(axis=-1, keepdims=True)
    i1 = jnp.where(logits == m1, lane, float(LANE)).min(axis=-1, keepdims=True)
    rest = jnp.where(lane == i1, NEG, logits)
    m2 = rest.max(axis=-1, keepdims=True)
    i2 = jnp.where(rest == m2, lane, float(LANE)).min(axis=-1, keepdims=True)
    e2 = jnp.exp(m2 - m1)
    g1 = 1.0 / (1.0 + e2)
    g2 = e2 * g1
    route_ref[...] = jnp.where(lane == 0, i1, jnp.where(lane == 1, i2, jnp.where(lane == 2, g1, jnp.where(lane == 3, g2, 0.0))))


def _merge(l, bu, att_ctx, na_lat, mla_lat, y, mods, w_conv, g_grp, w_out, g_post, g_pre_ff, router=None):
    moe = router is not None
    nb8 = TM // 8
    last8 = T_ALL // 8 - 1
    lat_rows = lambda w: pl.BlockSpec((TM, w), lambda i: (jnp.maximum(i - N_CTX_TILES, 0), 0))
    in_specs = [
        _rows(2 * CONV_W),
        pl.BlockSpec((8, CONV_W), lambda i: (jnp.maximum(i * nb8 - 1, 0), 1)),
        pl.BlockSpec((8, CONV_W), lambda i: (jnp.minimum((i + 1) * nb8, last8), 1)),
        pl.BlockSpec((TM, NA_W + MLA_W), lambda i: (jnp.minimum(i, N_CTX_TILES - 1), 0)),
        lat_rows(NA_W),
        lat_rows(MLA_W),
        _rows(D),
        _mod_spec(l),
        _layer_spec(l, (3, CONV_W)),
        _layer_spec(l, (1, D)),
        _layer_spec(l, (D, D)),
        _layer_spec(l, (1, D)),
        _layer_spec(l, (1, D)),
    ]
    args = [bu, bu, bu, att_ctx, na_lat, mla_lat, y, mods, w_conv, g_grp, w_out, g_post, g_pre_ff]
    out_specs = [_rows(D), _rows(D)]
    out_shape = [jax.ShapeDtypeStruct((T_ALL, D), F32), jax.ShapeDtypeStruct((T_ALL, D), F32 if moe else BF16)]
    if moe:
        li = l // 2
        in_specs += [_layer_spec(li, (D, LANE)), _layer_spec(li, (D, LANE)), _layer_spec(li, (1, LANE))]
        args += list(router)
        out_specs.append(_rows(LANE))
        out_shape.append(jax.ShapeDtypeStruct((T_ALL, LANE), F32))
    return pl.pallas_call(
        functools.partial(_merge_kernel, moe=moe),
        grid=(T_ALL // TM,),
        in_specs=in_specs,
        out_specs=out_specs,
        out_shape=out_shape,
        compiler_params=_params(("arbitrary",)),
        name="merge_outproj_moe" if moe else "merge_outproj",
    )(*args)


FF_CHUNK = D_FF // 2


def _ffn_kernel(h_ref, y_ref, mod_ref, w1_ref, w3_ref, w2_ref, gpost_ref, o_ref):
    h = h_ref[...]
    acc = None
    for c in range(D_FF // FF_CHUNK):
        sl = slice(c * FF_CHUNK, (c + 1) * FF_CHUNK)
        a1 = _dot(h, w1_ref[:, sl])
        a3 = _dot(h, w3_ref[:, sl])
        o = _dot((_silu(a1) * a3).astype(BF16), w2_ref[sl, :])
        acc = o if acc is None else acc + o
    o_ref[...] = y_ref[...] + mod_ref[5:6] * _rms(acc, gpost_ref[...])


def _dense_ffn(l, h, y, mods, w1, w3, w2, g_post):
    li = l // 2
    return pl.pallas_call(
        _ffn_kernel,
        grid=(T_ALL // TM,),
        in_specs=[_rows(D), _rows(D), _mod_spec(l), _layer_spec(li, (D, D_FF)), _layer_spec(li, (D, D_FF)),
                  _layer_spec(li, (D_FF, D)), _layer_spec(l, (1, D))],
        out_specs=_rows(D),
        out_shape=jax.ShapeDtypeStruct((T_ALL, D), F32),
        compiler_params=_params(("arbitrary",)),
        name="dense_ffn",
    )(h, y, mods, w1, w3, w2, g_post)


def _moe_kernel(ue_ref, us_ref, un_ref, pos_ref, h_hbm, w1_ref, w3_ref, w2_ref, z_hbm,
                xbuf, ybuf, wb1, wb3, wb2, tok_ref, dst_ref, sem_g, sem_s):
    u = pl.program_id(0)
    j = pl.program_id(1)
    nu = pl.num_programs(0)
    nj = pl.num_programs(1)
    slot = lax.rem(u, 2)
    n = un_ref[u]

    @pl.when(jnp.logical_and(u == 0, j == 0))
    def _():
        def invert(t4, c):
            for rr in range(8):
                tok = t4 * 4 + rr // 2
                p = pos_ref[t4 * 8 + rr]
                tok_ref[p] = tok
                dst_ref[p] = tok + (rr % 2) * T_ALL
            return c

        def tail(i, c):
            tok_ref[2 * T_ALL + i] = i
            dst_ref[2 * T_ALL + i] = i
            return c

        lax.fori_loop(0, T_ALL // 4, invert, 0)
        lax.fori_loop(0, MOE_U, tail, 0)

    def gather_copy(sl, r8, rs, tok):
        return pltpu.make_async_copy(h_hbm.at[pl.ds(tok, 1), :], xbuf.at[sl, r8, pl.ds(rs, 1), :], sem_g.at[sl])

    def scatter_copy(sl, r8, rs, dst):
        return pltpu.make_async_copy(ybuf.at[sl, r8, pl.ds(rs, 1), :], z_hbm.at[pl.ds(dst, 1), :], sem_s.at[sl])

    def for_rows(n_rows, fn):
        full = n_rows // MOE_ROW_UNROLL

        def body_many(g, c):
            for rr in range(MOE_ROW_UNROLL):
                fn(g * MOE_ROW_UNROLL + rr, g * (MOE_ROW_UNROLL // 8) + rr // 8, rr % 8)
            return c

        def body_one(r, c):
            fn(r, lax.shift_right_logical(r, 3), jnp.bitwise_and(r, 7))
            return c

        lax.fori_loop(0, full, body_many, 0)
        lax.fori_loop(full * MOE_ROW_UNROLL, n_rows, body_one, 0)

    def gather_rows(unit):
        return (un_ref[unit] + MOE_SUB - 1) // MOE_SUB * MOE_SUB

    def start_gather(unit, sl):
        base = us_ref[unit]
        for_rows(gather_rows(unit), lambda r, r8, rs: gather_copy(sl, r8, rs, tok_ref[base + r]).start())

    def wait_gather(unit, sl):
        for_rows(gather_rows(unit), lambda r, r8, rs: gather_copy(sl, 0, 0, 0).wait())

    def start_scatter(unit, sl):
        base = us_ref[unit]
        for_rows(un_ref[unit], lambda r, r8, rs: scatter_copy(sl, r8, rs, dst_ref[base + r]).start())

    def wait_scatter(unit, sl):
        for_rows(un_ref[unit], lambda r, r8, rs: scatter_copy(sl, 0, 0, 0).wait())

    @pl.when(jnp.logical_and(u == 0, j == 0))
    def _():
        start_gather(0, 0)

    @pl.when(j == 0)
    def _():
        wait_gather(u, slot)

        @pl.when(u + 1 < nu)
        def _():
            start_gather(u + 1, 1 - slot)

    def sub_tile(k, first):
        rows = pl.ds(k * (MOE_SUB // 8), MOE_SUB // 8)
        x = xbuf[slot, rows].reshape(MOE_SUB, D).astype(BF16)
        if k == 0:
            w1b, w3b, w2b = w1_ref[...].astype(BF16), w3_ref[...].astype(BF16), w2_ref[...].astype(BF16)
            wb1[...] = w1b
            wb3[...] = w3b
            wb2[...] = w2b
        else:
            w1b, w3b, w2b = wb1[...], wb3[...], wb2[...]
        o = _dot((_silu(_dot(x, w1b)) * _dot(x, w3b)).astype(BF16), w2b).reshape(MOE_SUB // 8, 8, D)
        if first:
            ybuf[slot, rows] = o
        else:
            ybuf[slot, rows] += o

    for k in range(MOE_U // MOE_SUB):
        live = n > k * MOE_SUB
        pl.when(jnp.logical_and(live, j == 0))(functools.partial(sub_tile, k, True))
        pl.when(jnp.logical_and(live, j > 0))(functools.partial(sub_tile, k, False))

    @pl.when(j == nj - 1)
    def _():
        @pl.when(u > 0)
        def _():
            wait_scatter(u - 1, 1 - slot)

        start_scatter(u, slot)

        @pl.when(u == nu - 1)
        def _():
            wait_scatter(u, slot)


def _moe_experts(li, unit_expert, unit_start, unit_rows, pos, h, w1, w3, w2):
    nj = D_FFE // MOE_TF

    def jj(u, j, un):
        return jnp.where(un[u] > 0, j, nj - 1)

    return pl.pallas_call(
        _moe_kernel,
        grid_spec=pltpu.PrefetchScalarGridSpec(
            num_scalar_prefetch=4,
            grid=(MOE_UNITS, nj),
            in_specs=[
                pl.BlockSpec(memory_space=pl.ANY),
                pl.BlockSpec((None, None, D, MOE_TF), lambda u, j, ue, us, un, *_: (li, ue[u], 0, jj(u, j, un))),
                pl.BlockSpec((None, None, D, MOE_TF), lambda u, j, ue, us, un, *_: (li, ue[u], 0, jj(u, j, un))),
                pl.BlockSpec((None, None, MOE_TF, D), lambda u, j, ue, us, un, *_: (li, ue[u], jj(u, j, un), 0)),
            ],
            out_specs=pl.BlockSpec(memory_space=pl.ANY),
            scratch_shapes=[
                pltpu.VMEM((2, MOE_U // 8, 8, D), F32), pltpu.VMEM((2, MOE_U // 8, 8, D), F32),
                pltpu.VMEM((D, MOE_TF), BF16), pltpu.VMEM((D, MOE_TF), BF16), pltpu.VMEM((MOE_TF, D), BF16),
                pltpu.SMEM((MOE_ASG_LEN,), jnp.int32), pltpu.SMEM((MOE_ASG_LEN,), jnp.int32),
                pltpu.SemaphoreType.DMA((2,)), pltpu.SemaphoreType.DMA((2,)),
            ],
        ),
        out_shape=jax.ShapeDtypeStruct((MOE_Z_ROWS, D), F32),
        compiler_params=_params(("arbitrary", "arbitrary"), disable_bounds_checks=True),
        name="moe_experts",
    )(unit_expert, unit_start, unit_rows, pos, h, w1, w3, w2)


def _moe_combine_kernel(z0_ref, z1_ref, route_ref, y_ref, mod_ref, gpost_ref, o_ref):
    r = route_ref[...]
    f = r[:, 2:3] * z0_ref[...] + r[:, 3:4] * z1_ref[...]
    o_ref[...] = y_ref[...] + mod_ref[5:6] * _rms(f, gpost_ref[...])


def _moe_combine(l, z, route, y, mods, g_post):
    return pl.pallas_call(
        _moe_combine_kernel,
        grid=(T_ALL // TM,),
        in_specs=[_rows(D), pl.BlockSpec((TM, D), lambda i: (T_ALL // TM + i, 0)), _rows(LANE), _rows(D),
                  _mod_spec(l), _layer_spec(l, (1, D))],
        out_specs=_rows(D),
        out_shape=jax.ShapeDtypeStruct((T_ALL, D), F32),
        compiler_params=_params(("arbitrary",)),
        name="moe_combine",
    )(z, z, route, y, mods, g_post)


def _moe_plan(route):
    e = route[:, 0:2].astype(jnp.int32).reshape(-1)
    experts = jnp.arange(N_EXP, dtype=jnp.int32)
    onehot = (e[:, None] == experts[None, :]).astype(jnp.int32)
    csum = jnp.cumsum(onehot, axis=0)
    counts = csum[-1]
    starts = jnp.cumsum(counts) - counts
    pos = jnp.sum((starts[None, :] + csum - 1) * onehot, axis=1)
    units_per = (counts + MOE_U - 1) // MOE_U
    unit_end = jnp.cumsum(units_per)
    total = unit_end[-1]
    uidx = jnp.arange(MOE_UNITS, dtype=jnp.int32)
    ue = jnp.minimum(jnp.sum((uidx[:, None] >= unit_end[None, :]).astype(jnp.int32), axis=1), N_EXP - 1)
    sel = (ue[:, None] == experts[None, :]).astype(jnp.int32)
    pick = lambda v: jnp.sum(sel * v[None, :], axis=1)
    k = uidx - pick(unit_end - units_per)
    valid = uidx < total
    unit_start = jnp.where(valid, pick(starts) + MOE_U * k, 0)
    unit_rows = jnp.where(valid, jnp.clip(pick(counts) - MOE_U * k, 0, MOE_U), 0)
    last_expert = jnp.sum(jnp.where(uidx == total - 1, ue, 0))
    unit_expert = jnp.where(valid, ue, last_expert)
    return (unit_expert.astype(jnp.int32), unit_start.astype(jnp.int32), unit_rows.astype(jnp.int32),
            pos.astype(jnp.int32))


def _rope_tables():
    half, quarter = ROPE // 2, ROPE // 4
    t = np.arange(LAT_L)
    pos = np.stack([t // GRID_W, t % GRID_W], axis=-1).astype(np.float32)
    inv_freq = np.power(np.float32(ROPE_BASE), -np.arange(quarter, dtype=np.float32) * np.float32(2.0) / np.float32(half))
    ang = pos[:, :, None] * inv_freq.astype(np.float32)
    ang = np.concatenate([ang, ang], axis=-1).reshape(LAT_L, ROPE).astype(np.float32)
    cos, sin = np.cos(ang).astype(np.float32), np.sin(ang).astype(np.float32)
    z64, z32 = np.zeros((LAT_L, NOPE), np.float32), np.zeros((LAT_L, LANE - NOPE - ROPE), np.float32)
    cosq = np.concatenate([np.ones((LAT_L, NOPE), np.float32), cos, z32], axis=1)
    cosk = np.concatenate([z64, cos, z32], axis=1)
    sin_t = np.concatenate([z64, sin, z32], axis=1)
    return jnp.asarray(cosq), jnp.asarray(sin_t), jnp.asarray(cosk), jnp.asarray(sin_t)


def _rot_cols(w):
    k = w.shape[:-1]
    w4 = w.reshape(k + (2, 2, ROPE // 4))
    return jnp.concatenate([-w4[..., 1:2, :], w4[..., 0:1, :]], axis=-2).reshape(k + (ROPE,))


def _prep_weights(w_in, w_qb, w_kvb, w_router):
    zeros = lambda *s: jnp.zeros(s, F32)
    w_kr = w_in[:, :, IN_MAIN:]
    w_in_p = jnp.concatenate([
        w_in[:, :, :IN_MAIN],
        zeros(DEPTH, D, NOPE), w_kr, zeros(DEPTH, D, LANE - NOPE - ROPE),
        zeros(DEPTH, D, NOPE), _rot_cols(w_kr), zeros(DEPTH, D, LANE - NOPE - ROPE)], axis=-1).astype(BF16)
    wq4 = w_qb.reshape(DEPTH, Q_RANK, MLA_H, NOPE + ROPE).transpose(0, 2, 1, 3)
    pad_q = zeros(DEPTH, MLA_H, Q_RANK, LANE - NOPE - ROPE)
    wq = jnp.concatenate([wq4, pad_q], axis=-1).astype(BF16)
    wqr = jnp.concatenate([zeros(DEPTH, MLA_H, Q_RANK, NOPE), _rot_cols(wq4[..., NOPE:]), pad_q], axis=-1).astype(BF16)
    wkv4 = w_kvb.reshape(DEPTH, KV_RANK, MLA_H, NOPE + VD).transpose(0, 2, 1, 3)
    wk = jnp.concatenate([wkv4[..., :NOPE], zeros(DEPTH, MLA_H, KV_RANK, LANE - NOPE)], axis=-1).astype(BF16)
    wv = wkv4[..., NOPE:]
    wv2 = jnp.concatenate([wv[:, 0::2], wv[:, 1::2]], axis=-1).astype(BF16)
    wr = jnp.pad(w_router, ((0, 0), (0, 0), (0, LANE - N_EXP)))
    wr_hi = wr.astype(BF16)
    wr_lo = (wr - wr_hi.astype(F32)).astype(BF16)
    return w_in_p, wq, wqr, wk, wv2, wr_hi, wr_lo


def kernel(x_prompt, x_sample, cache_na_k, cache_na_v, cache_mla_ckv, cache_mla_krope, c, c_ctx,
           w_in, w_conv, rpb, g_qa, w_qb, g_kva, w_kvb, g_grp, w_out, w_mod, b_mod,
           g_pre_mix, g_post_mix, g_pre_ff, g_post_ff, w_ff1, w_ff3, w_ff2,
           w_router, b_router, moe_w1, moe_w3, moe_w2):
    y = jnp.concatenate([x_prompt.reshape(T_CTX, D), x_sample.reshape(T_LAT, D)], axis=0)
    cond8 = jnp.concatenate([c_ctx[None, :], c, jnp.zeros((5, D), F32)], axis=0)
    mods = _modulation(cond8, w_mod, b_mod)[:, :3].reshape(DEPTH, 3, 6, D)

    w_in_p, wq, wqr, wk, wv2, wr_hi, wr_lo = _prep_weights(w_in, w_qb, w_kvb, w_router)
    w_out_b = w_out.astype(BF16)
    ff1, ff3, ff2 = w_ff1.astype(BF16), w_ff3.astype(BF16), w_ff2.astype(BF16)
    b_r = jnp.pad(b_router, ((0, 0), (0, LANE - N_EXP)))[:, None, :]
    vec = lambda a: a[:, None, :]
    g_pre_mix, g_post_mix, g_pre_ff, g_post_ff = vec(g_pre_mix), vec(g_post_mix), vec(g_pre_ff), vec(g_post_ff)
    g_qa, g_kva, g_grp = vec(g_qa), vec(g_kva), vec(g_grp)
    cosq, sinq, cosk, sink = _rope_tables()
    ones_t = jnp.ones((PAST, LANE), F32)
    zeros_t = jnp.zeros((PAST, LANE), F32)
    bias = _na_bias(rpb)
    kc = cache_na_k.reshape(N_LAT_B, DEPTH, PAST, NA_W)
    vc = cache_na_v.reshape(N_LAT_B, DEPTH, PAST, NA_W)
    c_kr = jnp.pad(cache_mla_krope, ((0, 0), (0, 0), (0, 0), (NOPE, 2 * LANE - NOPE - ROPE)))

    st_k, st_v, st_ckv, st_kr = [], [], [], []
    for l in range(DEPTH):
        bu, q, k, v, qa, ckv, kr = _in_projection(l, y, mods, g_pre_mix, w_in_p, g_qa, g_kva)
        st_k.append(k[:T_CTX])
        st_v.append(v[:T_CTX])
        st_ckv.append(ckv[:T_CTX])
        st_kr.append(kr[:T_CTX, NOPE:NOPE + ROPE])

        att_ctx = _ctx_attention(l, q, k, v, qa, ckv, kr, wq, wk, wv2)
        na_lat = _na_latent(l, q, k.reshape(3, GROUP, NA_W), v.reshape(3, GROUP, NA_W), kc, vc, bias)
        qh, kh, vh = _mla_prep_latent(l, qa, ckv, kr, cosq, sinq, cosk, sink, wq, wqr, wk, wv2)
        kch, vch = _mla_prep_cache(l, cache_mla_ckv, c_kr, ones_t, zeros_t, wk, wv2)
        mla_lat = _mla_attention(qh, kh, kch, vh, vch)

        li = l // 2
        if l % 2 == 0:
            y, h = _merge(l, bu, att_ctx, na_lat, mla_lat, y, mods, w_conv, g_grp, w_out_b, g_post_mix, g_pre_ff)
            y = _dense_ffn(l, h, y, mods, ff1, ff3, ff2, g_post_ff)
        else:
            y, h, route = _merge(l, bu, att_ctx, na_lat, mla_lat, y, mods, w_conv, g_grp, w_out_b, g_post_mix, g_pre_ff,
                                 router=(wr_hi, wr_lo, b_r))
            z = _moe_experts(li, *_moe_plan(route), h, moe_w1, moe_w3, moe_w2)
            y = _moe_combine(l, z, route, y, mods, g_post_ff)

    y_prompt = y[:T_CTX].reshape(N_CTX_B, CTX_L, D)
    y_sample = y[T_CTX:].reshape(N_LAT_B, LAT_L, D)
    stack = lambda xs, shp: jnp.stack([a.reshape((N_CTX_B, CTX_L) + shp) for a in xs], axis=1)
    return (y_prompt, y_sample, stack(st_k, (NA_H, NA_D)), stack(st_v, (NA_H, NA_D)),
            stack(st_ckv, (KV_RANK,)), stack(st_kr, (ROPE,)))
```

```python
import functools

import numpy as np
import jax
import jax.numpy as jnp
from jax import lax
from jax.experimental import pallas as pl
from jax.experimental.pallas import tpu as pltpu

F32 = jnp.float32
BF16 = jnp.bfloat16

D = 1024
DEPTH = 4
N_CTX_B, CTX_L = 16, 256
N_LAT_B, LAT_L = 2, 4096
GRID_W = 64
GRID_ROWS = LAT_L // GRID_W
GROUP = 4096
T_CTX = N_CTX_B * CTX_L
T_LAT = N_LAT_B * LAT_L
T_ALL = T_CTX + T_LAT
PAST = 256
CONV_W = 256
NA_H, NA_D = 6, 64
NA_W = NA_H * NA_D
WIN_R, WIN_C = 8, 16
RPB_R, RPB_C = 2 * WIN_R - 1, 2 * WIN_C - 1
MLA_H, NOPE, ROPE, VD = 6, 64, 32, 64
MLA_W = MLA_H * VD
Q_RANK, KV_RANK = 256, 128
IN_MAIN = 3 * CONV_W + 3 * NA_W + Q_RANK + KV_RANK
IN_PAD = IN_MAIN + 256
D_FF = 2816
N_EXP = 8
D_FFE = 3584
EPS = 1e-6
NEG = -1e30
ROPE_BASE = 10000.0
LANE = 128
VMEM_LIMIT = 56 * 1024 * 1024

TM = 512
MOE_U = 1024
MOE_SUB = 256
MOE_ROW_UNROLL = 32
MOE_TF = 896
MOE_UNITS = 2 * T_ALL // MOE_U + N_EXP
MOE_ASG_LEN = 2 * T_ALL + MOE_U
MOE_Z_ROWS = 2 * T_ALL


def _rms(x, g):
    return x * lax.rsqrt(jnp.mean(x * x, axis=-1, keepdims=True) + EPS) * g


def _silu(x):
    return x * (1.0 / (1.0 + jnp.exp(-x)))


def _dot(a, b):
    return jnp.dot(a, b, preferred_element_type=F32)


def _dot_nt(a, b):
    return lax.dot_general(a, b, (((1,), (1,)), ((), ())), preferred_element_type=F32)


def _params(sem, vmem=VMEM_LIMIT, **kw):
    return pltpu.CompilerParams(dimension_semantics=sem, vmem_limit_bytes=vmem, **kw)


def _layer_spec(l, shape):
    nd = len(shape)
    return pl.BlockSpec((None,) + tuple(shape), lambda *_: (l,) + (0,) * nd, pipeline_mode=pl.Buffered(1))


def _mod_spec(l):
    return pl.BlockSpec((None, None, 6, D), lambda i: (l, i * TM // GROUP, 0, 0))


def _rows(w):
    return pl.BlockSpec((TM, w), lambda i: (i, 0))


MOD_TN = 1536


def _mod_kernel(cond_ref, w_ref, b_ref, o_ref):
    s = _silu(cond_ref[...]).astype(BF16)
    o_ref[0] = _dot(s, w_ref[0].astype(BF16)) + b_ref[0]


def _modulation(cond8, w_mod, b_mod):
    return pl.pallas_call(
        _mod_kernel,
        grid=(DEPTH, 6 * D // MOD_TN),
        in_specs=[
            pl.BlockSpec((8, D), lambda l, n: (0, 0)),
            pl.BlockSpec((1, D, MOD_TN), lambda l, n: (l, 0, n)),
            pl.BlockSpec((1, 1, MOD_TN), lambda l, n: (l, 0, n)),
        ],
        out_specs=pl.BlockSpec((1, 8, MOD_TN), lambda l, n: (l, 0, n)),
        out_shape=jax.ShapeDtypeStruct((DEPTH, 8, 6 * D), F32),
        compiler_params=_params(("arbitrary", "arbitrary")),
        name="adaln_mod",
    )(cond8, w_mod, b_mod.reshape(DEPTH, 1, 6 * D))


def _inproj_kernel(y_ref, mod_ref, gpre_ref, w_ref, gqa_ref, gkva_ref,
                   bu_ref, q_ref, k_ref, v_ref, qa_ref, ckv_ref, kr_ref):
    mod = mod_ref[...]
    h = _rms(y_ref[...], gpre_ref[...]) * (1.0 + mod[1:2]) + mod[0:1]
    z = _dot(h.astype(BF16), w_ref[...])
    bu_ref[:, 0:CONV_W] = z[:, 0:CONV_W]
    bu_ref[:, CONV_W:2 * CONV_W] = z[:, CONV_W:2 * CONV_W] * z[:, 2 * CONV_W:3 * CONV_W]
    o = 3 * CONV_W
    q_ref[...] = (z[:, o:o + NA_W] * (NA_D ** -0.5 * LOG2_E)).astype(BF16)
    k_ref[...] = z[:, o + NA_W:o + 2 * NA_W]
    v_ref[...] = z[:, o + 2 * NA_W:o + 3 * NA_W]
    o += 3 * NA_W
    qa_ref[...] = _rms(z[:, o:o + Q_RANK], gqa_ref[...]).astype(BF16)
    o += Q_RANK
    ckv_ref[...] = _rms(z[:, o:o + KV_RANK], gkva_ref[...])
    kr_ref[...] = z[:, IN_MAIN:IN_PAD]


def _in_projection(l, y, mods, g_pre, w_in_p, g_qa, g_kva):
    shp = lambda w, dt: jax.ShapeDtypeStruct((T_ALL, w), dt)
    return pl.pallas_call(
        _inproj_kernel,
        grid=(T_ALL // TM,),
        in_specs=[_rows(D), _mod_spec(l), _layer_spec(l, (1, D)), _layer_spec(l, (D, IN_PAD)),
                  _layer_spec(l, (1, Q_RANK)), _layer_spec(l, (1, KV_RANK))],
        out_specs=[_rows(2 * CONV_W), _rows(NA_W), _rows(NA_W), _rows(NA_W), _rows(Q_RANK), _rows(KV_RANK),
                   _rows(2 * LANE)],
        out_shape=[shp(2 * CONV_W, F32), shp(NA_W, BF16), shp(NA_W, F32), shp(NA_W, F32),
                   shp(Q_RANK, BF16), shp(KV_RANK, F32), shp(2 * LANE, F32)],
        compiler_params=_params(("arbitrary",)),
        name="prenorm_inproj",
    )(y, mods, g_pre, w_in_p, g_qa, g_kva)


def _lane_lo():
    return lax.broadcasted_iota(jnp.int32, (1, LANE), 1) < NA_D


def _pair_attention(q2, keys, vals, biases):
    lo = _lane_lo()
    nk = len(keys)
    qm = [jnp.where(lo, q2, jnp.zeros_like(q2)), jnp.where(lo, jnp.zeros_like(q2), q2)]
    s = [[None] * nk for _ in range(2)]
    m = [None, None]
    l = [None, None]
    acc = [None, None]

    def scores(u, c):
        sc = _dot_nt(qm[u], keys[c])
        if biases[c] is not None:
            sc = sc + biases[c](u)
        s[u][c] = sc
        cm = sc.max(axis=-1, keepdims=True)
        m[u] = cm if m[u] is None else jnp.maximum(m[u], cm)

    def values(u, c):
        p = jnp.exp2(s[u][c] - m[u])
        ls = p.sum(axis=-1, keepdims=True)
        o = _dot(p.astype(BF16), vals[c])
        l[u] = ls if l[u] is None else l[u] + ls
        acc[u] = o if acc[u] is None else acc[u] + o

    for c in range(nk):
        scores(0, c)
    for c in range(nk):
        values(0, c)
        scores(1, c)
    for c in range(nk):
        values(1, c)
    return jnp.where(lo, acc[0] * (1.0 / l[0]), acc[1] * (1.0 / l[1]))


def _ctx_attn_kernel(q_ref, k_ref, v_ref, qa_ref, ckv_ref, kr_ref, wq_ref, wk_ref, wv_ref, o_ref):
    for j in range(NA_H // 2):
        sl = slice(LANE * j, LANE * (j + 1))
        k2 = k_ref[:, sl].astype(BF16)
        v2 = v_ref[:, sl].astype(BF16)
        o_ref[:, sl] = _pair_attention(q_ref[:, sl], [k2], [v2], [None]).astype(BF16)
    qa = qa_ref[...]
    ckv = ckv_ref[...].astype(BF16)
    kr = kr_ref[:, 0:LANE]
    scale = (NOPE + ROPE) ** -0.5 * LOG2_E
    lo = _lane_lo()
    for j in range(MLA_H // 2):
        v2 = _dot(ckv, wv_ref[j]).astype(BF16)
        outs = []
        for u in range(2):
            h = 2 * j + u
            qh = (_dot(qa, wq_ref[h]) * scale).astype(BF16)
            kh = (_dot(ckv, wk_ref[h]) + kr).astype(BF16)
            s = _dot_nt(qh, kh)
            m = s.max(axis=-1, keepdims=True)
            p = jnp.exp2(s - m)
            l = p.sum(axis=-1, keepdims=True)
            outs.append(_dot(p.astype(BF16), v2) * (1.0 / l))
        o_ref[:, NA_W + LANE * j:NA_W + LANE * (j + 1)] = jnp.where(lo, outs[0], outs[1]).astype(BF16)


def _ctx_attention(l, q, k, v, qa, ckv, kr, wq, wk, wv2):
    rows = lambda w: pl.BlockSpec((CTX_L, w), lambda b: (b, 0))
    return pl.pallas_call(
        _ctx_attn_kernel,
        grid=(N_CTX_B,),
        in_specs=[rows(NA_W), rows(NA_W), rows(NA_W), rows(Q_RANK), rows(KV_RANK), rows(2 * LANE),
                  _layer_spec(l, (MLA_H, Q_RANK, LANE)), _layer_spec(l, (MLA_H, KV_RANK, LANE)),
                  _layer_spec(l, (MLA_H // 2, KV_RANK, LANE))],
        out_specs=rows(NA_W + MLA_W),
        out_shape=jax.ShapeDtypeStruct((T_CTX, NA_W + MLA_W), BF16),
        compiler_params=_params(("arbitrary",)),
        name="ctx_attention",
    )(q, k, v, qa, ckv, kr, wq, wk, wv2)


NA_QROWS = 8
NA_KROWS = 16
NA_TQ = NA_QROWS * GRID_W
NA_TK = NA_KROWS * GRID_W
NA_RB = GRID_ROWS // NA_QROWS
NA_VARIANTS = ((0, 0), (NA_QROWS, NA_QROWS - WIN_R // 2), (GRID_ROWS - NA_QROWS, GRID_ROWS - NA_KROWS))


def _na_bias_kernel(rpb_ref, o_ref):
    l = pl.program_id(0)
    h = pl.program_id(1)
    lane = lax.broadcasted_iota(jnp.int32, (GRID_W, LANE), 1)
    qc = lax.broadcasted_iota(jnp.int32, (GRID_W, LANE), 0)
    kc = jnp.bitwise_and(lane, GRID_W - 1)
    dcol = jnp.clip(kc - qc, -(WIN_C - 1), WIN_C - 1) + (WIN_C - 1)
    cstart = jnp.clip(qc - WIN_C // 2, 0, GRID_W - WIN_C)
    col_ok = jnp.logical_and(kc >= cstart, kc < cstart + WIN_C)
    lo = lane < GRID_W
    hits = [dcol == b for b in range(RPB_C)]
    tabs = []
    for a in range(RPB_R):
        acc = jnp.zeros((GRID_W, LANE), F32)
        for b in range(RPB_C):
            acc = jnp.where(hits[b], rpb_ref[((l * RPB_R + a) * RPB_C + b) * NA_H + h], acc)
        tabs.append(jnp.where(col_ok, acc * LOG2_E, NEG))
    neg = jnp.full((GRID_W, LANE), NEG, F32)
    for vi, (r0, s0) in enumerate(NA_VARIANTS):
        for qi in range(NA_QROWS):
            qr = r0 + qi
            rs = min(max(qr - WIN_R // 2, 0), GRID_ROWS - WIN_R)
            for m in range(NA_KROWS // 2):
                kl = s0 + 2 * m
                left = tabs[kl - qr + WIN_R - 1] if rs <= kl < rs + WIN_R else neg
                right = tabs[kl + 1 - qr + WIN_R - 1] if rs <= kl + 1 < rs + WIN_R else neg
                blk = left if left is right else jnp.where(lo, left, right)
                o_ref[vi, GRID_W * qi:GRID_W * (qi + 1), LANE * m:LANE * (m + 1)] = blk


def _na_bias(rpb):
    return pl.pallas_call(
        _na_bias_kernel,
        grid=(DEPTH, NA_H),
        in_specs=[pl.BlockSpec(memory_space=pltpu.SMEM)],
        out_specs=pl.BlockSpec((None, 3, None, NA_TQ, NA_TK), lambda l, h: (l, 0, h, 0, 0)),
        out_shape=jax.ShapeDtypeStruct((DEPTH, 3, NA_H, NA_TQ, NA_TK), F32),
        compiler_params=_params(("arbitrary", "arbitrary")),
        name="na_bias_tiles",
    )(rpb.reshape(-1))


def _na_lat_kernel(q_ref, k_ref, v_ref, kc_ref, vc_ref, b_ref, o_ref):
    rb = pl.program_id(2)
    start = pl.multiple_of(jnp.clip(NA_QROWS * rb - WIN_R // 2, 0, GRID_ROWS - NA_KROWS) * GRID_W, 256)
    half = NA_TK // 2
    keys, vals, biases = [], [], []
    for c in range(2):
        keys.append(k_ref[pl.ds(start + c * half, half), :].astype(BF16))
        vals.append(v_ref[pl.ds(start + c * half, half), :].astype(BF16))
        biases.append(lambda u, c=c: b_ref[u, :, c * half:(c + 1) * half])
    keys.append(kc_ref[...].astype(BF16))
    vals.append(vc_ref[...].astype(BF16))
    biases.append(None)
    o_ref[...] = _pair_attention(q_ref[...], keys, vals, biases).astype(BF16)


def _na_latent(l, q, k3, v3, kc, vc, bias):
    def variant(rb):
        return jnp.where(rb == 0, 0, jnp.where(rb == NA_RB - 1, 2, 1))

    qoff = T_CTX // NA_TQ
    return pl.pallas_call(
        _na_lat_kernel,
        grid=(N_LAT_B, NA_H // 2, NA_RB),
        in_specs=[
            pl.BlockSpec((NA_TQ, LANE), lambda b, j, r: (qoff + b * NA_RB + r, j)),
            pl.BlockSpec((None, LAT_L, LANE), lambda b, j, r: (1 + b, 0, j)),
            pl.BlockSpec((None, LAT_L, LANE), lambda b, j, r: (1 + b, 0, j)),
            pl.BlockSpec((None, None, PAST, LANE), lambda b, j, r: (b, l, 0, j)),
            pl.BlockSpec((None, None, PAST, LANE), lambda b, j, r: (b, l, 0, j)),
            pl.BlockSpec((None, None, 2, NA_TQ, NA_TK), lambda b, j, r: (l, variant(r), j, 0, 0)),
        ],
        out_specs=pl.BlockSpec((NA_TQ, LANE), lambda b, j, r: (b * NA_RB + r, j)),
        out_shape=jax.ShapeDtypeStruct((T_LAT, NA_W), BF16),
        compiler_params=_params(("arbitrary", "arbitrary", "arbitrary")),
        name="na_latent",
    )(q, k3, v3, kc, vc, bias)


MLA_TP = 512


def _mla_prep_kernel(*refs, with_q):
    if with_q:
        (qa_ref, ckv_ref, kr_ref, cq_ref, sq_ref, ck_ref, sk_ref,
         wq_ref, wqr_ref, wk_ref, wv_ref, qo_ref, ko_ref, vo_ref) = refs
    else:
        ckv_ref, kr_ref, ck_ref, sk_ref, wk_ref, wv_ref, ko_ref, vo_ref = refs
    ckv = ckv_ref[...].astype(BF16)
    krope = kr_ref[:, 0:LANE] * ck_ref[...] + kr_ref[:, LANE:2 * LANE] * sk_ref[...]
    for h in range(MLA_H):
        ko_ref[h] = (_dot(ckv, wk_ref[h]) + krope).astype(BF16)
    for j in range(MLA_H // 2):
        vo_ref[j] = _dot(ckv, wv_ref[j]).T.astype(BF16)
    if with_q:
        qa = qa_ref[...]
        scale = (NOPE + ROPE) ** -0.5 * LOG2_E
        for h in range(MLA_H):
            qh = _dot(qa, wq_ref[h]) * cq_ref[...] + _dot(qa, wqr_ref[h]) * sq_ref[...]
            qo_ref[h] = (qh * scale).astype(BF16)


def _mla_prep_latent(l, qa, ckv, kr, cosq, sinq, cosk, sink, wq, wqr, wk, wv2):
    off = T_CTX // MLA_TP
    nblk = LAT_L // MLA_TP
    rows = lambda w: pl.BlockSpec((MLA_TP, w), lambda b, i: (off + b * nblk + i, 0))
    tab = pl.BlockSpec((MLA_TP, LANE), lambda b, i: (i, 0))
    outh = lambda n: pl.BlockSpec((None, n, MLA_TP, LANE), lambda b, i: (b, 0, i, 0))
    return pl.pallas_call(
        functools.partial(_mla_prep_kernel, with_q=True),
        grid=(N_LAT_B, nblk),
        in_specs=[rows(Q_RANK), rows(KV_RANK), rows(2 * LANE), tab, tab, tab, tab,
                  _layer_spec(l, (MLA_H, Q_RANK, LANE)), _layer_spec(l, (MLA_H, Q_RANK, LANE)),
                  _layer_spec(l, (MLA_H, KV_RANK, LANE)), _layer_spec(l, (MLA_H // 2, KV_RANK, LANE))],
        out_specs=[outh(MLA_H), outh(MLA_H),
                   pl.BlockSpec((None, MLA_H // 2, LANE, MLA_TP), lambda b, i: (b, 0, 0, i))],
        out_shape=[jax.ShapeDtypeStruct((N_LAT_B, MLA_H, LAT_L, LANE), BF16),
                   jax.ShapeDtypeStruct((N_LAT_B, MLA_H, LAT_L, LANE), BF16),
                   jax.ShapeDtypeStruct((N_LAT_B, MLA_H // 2, LANE, LAT_L), BF16)],
        compiler_params=_params(("arbitrary", "arbitrary")),
        name="mla_prep_latent",
    )(qa, ckv, kr, cosq, sinq, cosk, sink, wq, wqr, wk, wv2)


def _mla_prep_cache(l, ckv, kr, ones, zeros, wk, wv2):
    rows = lambda w: pl.BlockSpec((None, None, PAST, w), lambda b: (b, l, 0, 0))
    tab = pl.BlockSpec((PAST, LANE), lambda b: (0, 0))
    outh = lambda n: pl.BlockSpec((None, n, PAST, LANE), lambda b: (b, 0, 0, 0))
    return pl.pallas_call(
        functools.partial(_mla_prep_kernel, with_q=False),
        grid=(N_LAT_B,),
        in_specs=[rows(KV_RANK), rows(2 * LANE), tab, tab,
                  _layer_spec(l, (MLA_H, KV_RANK, LANE)), _layer_spec(l, (MLA_H // 2, KV_RANK, LANE))],
        out_specs=[outh(MLA_H), pl.BlockSpec((None, MLA_H // 2, LANE, PAST), lambda b: (b, 0, 0, 0))],
        out_shape=[jax.ShapeDtypeStruct((N_LAT_B, MLA_H, PAST, LANE), BF16),
                   jax.ShapeDtypeStruct((N_LAT_B, MLA_H // 2, LANE, PAST), BF16)],
        compiler_params=_params(("arbitrary",)),
        name="mla_prep_cache",
    )(ckv, kr, ones, zeros, wk, wv2)


MLA_TQ = 256
MLA_KC = 256
LOG2_E = 1.4426950408889634


def _mla_attn_kernel(q_ref, k_ref, kc_ref, vt_ref, vct_ref, o_ref):
    lo = _lane_lo()
    chunks = [(k_ref, vt_ref, c * MLA_KC, MLA_KC) for c in range(LAT_L // MLA_KC)] + [(kc_ref, vct_ref, 0, PAST)]
    nc = len(chunks)
    p = [[None] * nc for _ in range(MLA_H)]
    ps = [[None] * nc for _ in range(MLA_H)]
    mc = [[None] * nc for _ in range(MLA_H)]
    l = [None] * MLA_H
    acc = [None] * MLA_H

    def scores(h, c):
        kref, _, start, size = chunks[c]
        s = _dot_nt(kref[h, start:start + size, :], q_ref[h])
        cm = s.max(axis=0, keepdims=True)
        mc[h][c] = cm if c == 0 else jnp.maximum(mc[h][c - 1], cm)
        e = jnp.exp2(s - mc[h][c])
        ps[h][c] = e.sum(axis=0, keepdims=True)
        p[h][c] = e.astype(BF16)

    def values(h, c):
        _, vref, start, size = chunks[c]
        a = jnp.exp2(mc[h][c] - mc[h][nc - 1])
        o = _dot(vref[h // 2, :, start:start + size], p[h][c]) * a
        ls = ps[h][c] * a
        l[h] = ls if l[h] is None else l[h] + ls
        acc[h] = o if acc[h] is None else acc[h] + o

    for h in range(MLA_H + 1):
        for c in range(nc):
            if h > 0:
                values(h - 1, c)
            if h < MLA_H:
                scores(h, c)
    outs = [(acc[h] * (1.0 / l[h])).T for h in range(MLA_H)]
    for j in range(MLA_H // 2):
        o_ref[:, LANE * j:LANE * (j + 1)] = jnp.where(lo, outs[2 * j], outs[2 * j + 1]).astype(BF16)


def _mla_attention(qh, kh, kch, vh, vch):
    nq = LAT_L // MLA_TQ
    whole = lambda *shape: pl.BlockSpec((None,) + shape, lambda b, i: (b,) + (0,) * len(shape))
    return pl.pallas_call(
        _mla_attn_kernel,
        grid=(N_LAT_B, nq),
        in_specs=[
            pl.BlockSpec((None, MLA_H, MLA_TQ, LANE), lambda b, i: (b, 0, i, 0)),
            whole(MLA_H, LAT_L, LANE), whole(MLA_H, PAST, LANE),
            whole(MLA_H // 2, LANE, LAT_L), whole(MLA_H // 2, LANE, PAST),
        ],
        out_specs=pl.BlockSpec((MLA_TQ, MLA_W), lambda b, i: (b * nq + i, 0)),
        out_shape=jax.ShapeDtypeStruct((T_LAT, MLA_W), BF16),
        compiler_params=_params(("arbitrary", "arbitrary")),
        name="mla_attention",
    )(qh, kh, kch, vh, vch)


N_CTX_TILES = T_CTX // TM


def _merge_kernel(*refs, moe):
    if moe:
        (bu_ref, up_ref, un_ref, attc_ref, na_ref, mla_ref, y_ref, mod_ref, wconv_ref, ggrp_ref, wout_ref, gpost_ref,
         gpre_ref, wr_hi_ref, wr_lo_ref, br_ref, yo_ref, h_ref, route_ref) = refs
    else:
        (bu_ref, up_ref, un_ref, attc_ref, na_ref, mla_ref, y_ref, mod_ref, wconv_ref, ggrp_ref, wout_ref, gpost_ref,
         gpre_ref, yo_ref, h_ref) = refs
    i = pl.program_id(0)
    local = lax.broadcasted_iota(jnp.int32, (TM, 1), 0)
    row = i * TM + local
    seq_mask = jnp.where(row < T_CTX, CTX_L - 1, LAT_L - 1)
    pos = jnp.bitwise_and(row, seq_mask)
    u = bu_ref[:, CONV_W:2 * CONV_W]
    prev = jnp.where(local == 0, up_ref[7:8, :], pltpu.roll(u, 1, axis=0))
    prev = jnp.where(pos == 0, 0.0, prev)
    nxt = jnp.where(local == TM - 1, un_ref[0:1, :], pltpu.roll(u, TM - 1, axis=0))
    nxt = jnp.where(pos == seq_mask, 0.0, nxt)
    wc = wconv_ref[...]
    conv = bu_ref[:, 0:CONV_W] * (wc[0:1] * prev + wc[1:2] * u + wc[2:3] * nxt)
    gg = ggrp_ref[...]
    is_ctx = i < N_CTX_TILES
    att_na = jnp.where(is_ctx, attc_ref[:, 0:NA_W], na_ref[...]).astype(F32)
    att_mla = jnp.where(is_ctx, attc_ref[:, NA_W:], mla_ref[...]).astype(F32)
    mixin = jnp.concatenate([
        _rms(conv, gg[:, 0:CONV_W]).astype(BF16),
        _rms(att_na, gg[:, CONV_W:CONV_W + NA_W]).astype(BF16),
        _rms(att_mla, gg[:, CONV_W + NA_W:]).astype(BF16)], axis=-1)
    mix = _dot(mixin, wout_ref[...])
    mod = mod_ref[...]
    y = y_ref[...] + mod[2:3] * _rms(mix, gpost_ref[...])
    yo_ref[...] = y
    h = _rms(y, gpre_ref[...]) * (1.0 + mod[4:5]) + mod[3:4]
    if not moe:
        h_ref[...] = h.astype(BF16)
        return
    h_ref[...] = h
    hi = h.astype(BF16)
    lo = (h - hi.astype(F32)).astype(BF16)
    logits = _dot(hi, wr_hi_ref[...]) + (_dot(lo, wr_hi_ref[...]) + _dot(hi, wr_lo_ref[...])) + br_ref[...]
    lane = lax.broadcasted_iota(jnp.int32, (TM, LANE), 1).astype(F32)
    logits = jnp.where(lane < N_EXP, logits, NEG)
    m1 = logits.max(axis=-1, keepdims=True)
    i1 = jnp.where(logits == m1, lane, float(LANE)).min(axis=-1, keepdims=True)
    rest = jnp.where(lane == i1, NEG, logits)
    m2 = rest.max(axis=-1, keepdims=True)
    i2 = jnp.where(rest == m2, lane, float(LANE)).min(axis=-1, keepdims=True)
    e2 = jnp.exp(m2 - m1)
    g1 = 1.0 / (1.0 + e2)
    g2 = e2 * g1
    route_ref[...] = jnp.where(lane == 0, i1, jnp.where(lane == 1, i2, jnp.where(lane == 2, g1, jnp.where(lane == 3, g2, 0.0))))


def _merge(l, bu, att_ctx, na_lat, mla_lat, y, mods, w_conv, g_grp, w_out, g_post, g_pre_ff, router=None):
    moe = router is not None
    nb8 = TM // 8
    last8 = T_ALL // 8 - 1
    lat_rows = lambda w: pl.BlockSpec((TM, w), lambda i: (jnp.maximum(i - N_CTX_TILES, 0), 0))
    in_specs = [
        _rows(2 * CONV_W),
        pl.BlockSpec((8, CONV_W), lambda i: (jnp.maximum(i * nb8 - 1, 0), 1)),
        pl.BlockSpec((8, CONV_W), lambda i: (jnp.minimum((i + 1) * nb8, last8), 1)),
        pl.BlockSpec((TM, NA_W + MLA_W), lambda i: (jnp.minimum(i, N_CTX_TILES - 1), 0)),
        lat_rows(NA_W),
        lat_rows(MLA_W),
        _rows(D),
        _mod_spec(l),
        _layer_spec(l, (3, CONV_W)),
        _layer_spec(l, (1, D)),
        _layer_spec(l, (D, D)),
        _layer_spec(l, (1, D)),
        _layer_spec(l, (1, D)),
    ]
    args = [bu, bu, bu, att_ctx, na_lat, mla_lat, y, mods, w_conv, g_grp, w_out, g_post, g_pre_ff]
    out_specs = [_rows(D), _rows(D)]
    out_shape = [jax.ShapeDtypeStruct((T_ALL, D), F32), jax.ShapeDtypeStruct((T_ALL, D), F32 if moe else BF16)]
    if moe:
        li = l // 2
        in_specs += [_layer_spec(li, (D, LANE)), _layer_spec(li, (D, LANE)), _layer_spec(li, (1, LANE))]
        args += list(router)
        out_specs.append(_rows(LANE))
        out_shape.append(jax.ShapeDtypeStruct((T_ALL, LANE), F32))
    return pl.pallas_call(
        functools.partial(_merge_kernel, moe=moe),
        grid=(T_ALL // TM,),
        in_specs=in_specs,
        out_specs=out_specs,
        out_shape=out_shape,
        compiler_params=_params(("arbitrary",)),
        name="merge_outproj_moe" if moe else "merge_outproj",
    )(*args)


FF_CHUNK = D_FF // 2


def _ffn_kernel(h_ref, y_ref, mod_ref, w1_ref, w3_ref, w2_ref, gpost_ref, o_ref):
    h = h_ref[...]
    acc = None
    for c in range(D_FF // FF_CHUNK):
        sl = slice(c * FF_CHUNK, (c + 1) * FF_CHUNK)
        a1 = _dot(h, w1_ref[:, sl])
        a3 = _dot(h, w3_ref[:, sl])
        o = _dot((_silu(a1) * a3).astype(BF16), w2_ref[sl, :])
        acc = o if acc is None else acc + o
    o_ref[...] = y_ref[...] + mod_ref[5:6] * _rms(acc, gpost_ref[...])


def _dense_ffn(l, h, y, mods, w1, w3, w2, g_post):
    li = l // 2
    return pl.pallas_call(
        _ffn_kernel,
        grid=(T_ALL // TM,),
        in_specs=[_rows(D), _rows(D), _mod_spec(l), _layer_spec(li, (D, D_FF)), _layer_spec(li, (D, D_FF)),
                  _layer_spec(li, (D_FF, D)), _layer_spec(l, (1, D))],
        out_specs=_rows(D),
        out_shape=jax.ShapeDtypeStruct((T_ALL, D), F32),
        compiler_params=_params(("arbitrary",)),
        name="dense_ffn",
    )(h, y, mods, w1, w3, w2, g_post)


def _moe_kernel(ue_ref, us_ref, un_ref, pos_ref, h_hbm, w1_ref, w3_ref, w2_ref, z_hbm,
                xbuf, ybuf, wb1, wb3, wb2, tok_ref, dst_ref, sem_g, sem_s):
    u = pl.program_id(0)
    j = pl.program_id(1)
    nu = pl.num_programs(0)
    nj = pl.num_programs(1)
    slot = lax.rem(u, 2)
    n = un_ref[u]

    @pl.when(jnp.logical_and(u == 0, j == 0))
    def _():
        def invert(t4, c):
            for rr in range(8):
                tok = t4 * 4 + rr // 2
                p = pos_ref[t4 * 8 + rr]
                tok_ref[p] = tok
                dst_ref[p] = tok + (rr % 2) * T_ALL
            return c

        def tail(i, c):
            tok_ref[2 * T_ALL + i] = i
            dst_ref[2 * T_ALL + i] = i
            return c

        lax.fori_loop(0, T_ALL // 4, invert, 0)
        lax.fori_loop(0, MOE_U, tail, 0)

    def gather_copy(sl, r8, rs, tok):
        return pltpu.make_async_copy(h_hbm.at[pl.ds(tok, 1), :], xbuf.at[sl, r8, pl.ds(rs, 1), :], sem_g.at[sl])

    def scatter_copy(sl, r8, rs, dst):
        return pltpu.make_async_copy(ybuf.at[sl, r8, pl.ds(rs, 1), :], z_hbm.at[pl.ds(dst, 1), :], sem_s.at[sl])

    def for_rows(n_rows, fn):
        full = n_rows // MOE_ROW_UNROLL

        def body_many(g, c):
            for rr in range(MOE_ROW_UNROLL):
                fn(g * MOE_ROW_UNROLL + rr, g * (MOE_ROW_UNROLL // 8) + rr // 8, rr % 8)
            return c

        def body_one(r, c):
            fn(r, lax.shift_right_logical(r, 3), jnp.bitwise_and(r, 7))
            return c

        lax.fori_loop(0, full, body_many, 0)
        lax.fori_loop(full * MOE_ROW_UNROLL, n_rows, body_one, 0)

    def gather_rows(unit):
        return (un_ref[unit] + MOE_SUB - 1) // MOE_SUB * MOE_SUB

    def start_gather(unit, sl):
        base = us_ref[unit]
        for_rows(gather_rows(unit), lambda r, r8, rs: gather_copy(sl, r8, rs, tok_ref[base + r]).start())

    def wait_gather(unit, sl):
        for_rows(gather_rows(unit), lambda r, r8, rs: gather_copy(sl, 0, 0, 0).wait())

    def start_scatter(unit, sl):
        base = us_ref[unit]
        for_rows(un_ref[unit], lambda r, r8, rs: scatter_copy(sl, r8, rs, dst_ref[base + r]).start(
            priority=rs % 2 if isinstance(rs, int) else 0))

    def wait_scatter(unit, sl):
        for_rows(un_ref[unit], lambda r, r8, rs: scatter_copy(sl, 0, 0, 0).wait())

    @pl.when(jnp.logical_and(u == 0, j == 0))
    def _():
        start_gather(0, 0)

    @pl.when(j == 0)
    def _():
        wait_gather(u, slot)

        @pl.when(u + 1 < nu)
        def _():
            start_gather(u + 1, 1 - slot)

    def sub_tile(k, first):
        rows = pl.ds(k * (MOE_SUB // 8), MOE_SUB // 8)
        x = xbuf[slot, rows].reshape(MOE_SUB, D).astype(BF16)
        if k == 0:
            w1b, w3b, w2b = w1_ref[...].astype(BF16), w3_ref[...].astype(BF16), w2_ref[...].astype(BF16)
            wb1[...] = w1b
            wb3[...] = w3b
            wb2[...] = w2b
        else:
            w1b, w3b, w2b = wb1[...], wb3[...], wb2[...]
        o = _dot((_silu(_dot(x, w1b)) * _dot(x, w3b)).astype(BF16), w2b).reshape(MOE_SUB // 8, 8, D)
        if first:
            ybuf[slot, rows] = o
        else:
            ybuf[slot, rows] += o

    for k in range(MOE_U // MOE_SUB):
        live = n > k * MOE_SUB
        pl.when(jnp.logical_and(live, j == 0))(functools.partial(sub_tile, k, True))
        pl.when(jnp.logical_and(live, j > 0))(functools.partial(sub_tile, k, False))

    @pl.when(j == nj - 1)
    def _():
        @pl.when(u > 0)
        def _():
            wait_scatter(u - 1, 1 - slot)

        start_scatter(u, slot)

        @pl.when(u == nu - 1)
        def _():
            wait_scatter(u, slot)


def _moe_experts(li, unit_expert, unit_start, unit_rows, pos, h, w1, w3, w2):
    nj = D_FFE // MOE_TF

    def jj(u, j, un):
        return jnp.where(un[u] > 0, j, nj - 1)

    return pl.pallas_call(
        _moe_kernel,
        grid_spec=pltpu.PrefetchScalarGridSpec(
            num_scalar_prefetch=4,
            grid=(MOE_UNITS, nj),
            in_specs=[
                pl.BlockSpec(memory_space=pl.ANY),
                pl.BlockSpec((None, None, D, MOE_TF), lambda u, j, ue, us, un, *_: (li, ue[u], 0, jj(u, j, un))),
                pl.BlockSpec((None, None, D, MOE_TF), lambda u, j, ue, us, un, *_: (li, ue[u], 0, jj(u, j, un))),
                pl.BlockSpec((None, None, MOE_TF, D), lambda u, j, ue, us, un, *_: (li, ue[u], jj(u, j, un), 0)),
            ],
            out_specs=pl.BlockSpec(memory_space=pl.ANY),
            scratch_shapes=[
                pltpu.VMEM((2, MOE_U // 8, 8, D), F32), pltpu.VMEM((2, MOE_U // 8, 8, D), F32),
                pltpu.VMEM((D, MOE_TF), BF16), pltpu.VMEM((D, MOE_TF), BF16), pltpu.VMEM((MOE_TF, D), BF16),
                pltpu.SMEM((MOE_ASG_LEN,), jnp.int32), pltpu.SMEM((MOE_ASG_LEN,), jnp.int32),
                pltpu.SemaphoreType.DMA((2,)), pltpu.SemaphoreType.DMA((2,)),
            ],
        ),
        out_shape=jax.ShapeDtypeStruct((MOE_Z_ROWS, D), F32),
        compiler_params=_params(("arbitrary", "arbitrary"), disable_bounds_checks=True),
        name="moe_experts",
    )(unit_expert, unit_start, unit_rows, pos, h, w1, w3, w2)


def _moe_combine_kernel(z0_ref, z1_ref, route_ref, y_ref, mod_ref, gpost_ref, o_ref):
    r = route_ref[...]
    f = r[:, 2:3] * z0_ref[...] + r[:, 3:4] * z1_ref[...]
    o_ref[...] = y_ref[...] + mod_ref[5:6] * _rms(f, gpost_ref[...])


def _moe_combine(l, z, route, y, mods, g_post):
    return pl.pallas_call(
        _moe_combine_kernel,
        grid=(T_ALL // TM,),
        in_specs=[_rows(D), pl.BlockSpec((TM, D), lambda i: (T_ALL // TM + i, 0)), _rows(LANE), _rows(D),
                  _mod_spec(l), _layer_spec(l, (1, D))],
        out_specs=_rows(D),
        out_shape=jax.ShapeDtypeStruct((T_ALL, D), F32),
        compiler_params=_params(("arbitrary",)),
        name="moe_combine",
    )(z, z, route, y, mods, g_post)


def _moe_plan(route):
    e = route[:, 0:2].astype(jnp.int32).reshape(-1)
    experts = jnp.arange(N_EXP, dtype=jnp.int32)
    onehot = (e[:, None] == experts[None, :]).astype(jnp.int32)
    csum = jnp.cumsum(onehot, axis=0)
    counts = csum[-1]
    starts = jnp.cumsum(counts) - counts
    pos = jnp.sum((starts[None, :] + csum - 1) * onehot, axis=1)
    units_per = (counts + MOE_U - 1) // MOE_U
    unit_end = jnp.cumsum(units_per)
    total = unit_end[-1]
    uidx = jnp.arange(MOE_UNITS, dtype=jnp.int32)
    ue = jnp.minimum(jnp.sum((uidx[:, None] >= unit_end[None, :]).astype(jnp.int32), axis=1), N_EXP - 1)
    sel = (ue[:, None] == experts[None, :]).astype(jnp.int32)
    pick = lambda v: jnp.sum(sel * v[None, :], axis=1)
    k = uidx - pick(unit_end - units_per)
    valid = uidx < total
    unit_start = jnp.where(valid, pick(starts) + MOE_U * k, 0)
    unit_rows = jnp.where(valid, jnp.clip(pick(counts) - MOE_U * k, 0, MOE_U), 0)
    last_expert = jnp.sum(jnp.where(uidx == total - 1, ue, 0))
    unit_expert = jnp.where(valid, ue, last_expert)
    return (unit_expert.astype(jnp.int32), unit_start.astype(jnp.int32), unit_rows.astype(jnp.int32),
            pos.astype(jnp.int32))


def _rope_tables():
    half, quarter = ROPE // 2, ROPE // 4
    t = np.arange(LAT_L)
    pos = np.stack([t // GRID_W, t % GRID_W], axis=-1).astype(np.float32)
    inv_freq = np.power(np.float32(ROPE_BASE), -np.arange(quarter, dtype=np.float32) * np.float32(2.0) / np.float32(half))
    ang = pos[:, :, None] * inv_freq.astype(np.float32)
    ang = np.concatenate([ang, ang], axis=-1).reshape(LAT_L, ROPE).astype(np.float32)
    cos, sin = np.cos(ang).astype(np.float32), np.sin(ang).astype(np.float32)
    z64, z32 = np.zeros((LAT_L, NOPE), np.float32), np.zeros((LAT_L, LANE - NOPE - ROPE), np.float32)
    cosq = np.concatenate([np.ones((LAT_L, NOPE), np.float32), cos, z32], axis=1)
    cosk = np.concatenate([z64, cos, z32], axis=1)
    sin_t = np.concatenate([z64, sin, z32], axis=1)
    return jnp.asarray(cosq), jnp.asarray(sin_t), jnp.asarray(cosk), jnp.asarray(sin_t)


def _rot_cols(w):
    k = w.shape[:-1]
    w4 = w.reshape(k + (2, 2, ROPE // 4))
    return jnp.concatenate([-w4[..., 1:2, :], w4[..., 0:1, :]], axis=-2).reshape(k + (ROPE,))


def _prep_weights(w_in, w_qb, w_kvb, w_router):
    zeros = lambda *s: jnp.zeros(s, F32)
    w_kr = w_in[:, :, IN_MAIN:]
    w_in_p = jnp.concatenate([
        w_in[:, :, :IN_MAIN],
        zeros(DEPTH, D, NOPE), w_kr, zeros(DEPTH, D, LANE - NOPE - ROPE),
        zeros(DEPTH, D, NOPE), _rot_cols(w_kr), zeros(DEPTH, D, LANE - NOPE - ROPE)], axis=-1).astype(BF16)
    wq4 = w_qb.reshape(DEPTH, Q_RANK, MLA_H, NOPE + ROPE).transpose(0, 2, 1, 3)
    pad_q = zeros(DEPTH, MLA_H, Q_RANK, LANE - NOPE - ROPE)
    wq = jnp.concatenate([wq4, pad_q], axis=-1).astype(BF16)
    wqr = jnp.concatenate([zeros(DEPTH, MLA_H, Q_RANK, NOPE), _rot_cols(wq4[..., NOPE:]), pad_q], axis=-1).astype(BF16)
    wkv4 = w_kvb.reshape(DEPTH, KV_RANK, MLA_H, NOPE + VD).transpose(0, 2, 1, 3)
    wk = jnp.concatenate([wkv4[..., :NOPE], zeros(DEPTH, MLA_H, KV_RANK, LANE - NOPE)], axis=-1).astype(BF16)
    wv = wkv4[..., NOPE:]
    wv2 = jnp.concatenate([wv[:, 0::2], wv[:, 1::2]], axis=-1).astype(BF16)
    wr = jnp.pad(w_router, ((0, 0), (0, 0), (0, LANE - N_EXP)))
    wr_hi = wr.astype(BF16)
    wr_lo = (wr - wr_hi.astype(F32)).astype(BF16)
    return w_in_p, wq, wqr, wk, wv2, wr_hi, wr_lo


def kernel(x_prompt, x_sample, cache_na_k, cache_na_v, cache_mla_ckv, cache_mla_krope, c, c_ctx,
           w_in, w_conv, rpb, g_qa, w_qb, g_kva, w_kvb, g_grp, w_out, w_mod, b_mod,
           g_pre_mix, g_post_mix, g_pre_ff, g_post_ff, w_ff1, w_ff3, w_ff2,
           w_router, b_router, moe_w1, moe_w3, moe_w2):
    y = jnp.concatenate([x_prompt.reshape(T_CTX, D), x_sample.reshape(T_LAT, D)], axis=0)
    cond8 = jnp.concatenate([c_ctx[None, :], c, jnp.zeros((5, D), F32)], axis=0)
    mods = _modulation(cond8, w_mod, b_mod)[:, :3].reshape(DEPTH, 3, 6, D)

    w_in_p, wq, wqr, wk, wv2, wr_hi, wr_lo = _prep_weights(w_in, w_qb, w_kvb, w_router)
    w_out_b = w_out.astype(BF16)
    ff1, ff3, ff2 = w_ff1.astype(BF16), w_ff3.astype(BF16), w_ff2.astype(BF16)
    b_r = jnp.pad(b_router, ((0, 0), (0, LANE - N_EXP)))[:, None, :]
    vec = lambda a: a[:, None, :]
    g_pre_mix, g_post_mix, g_pre_ff, g_post_ff = vec(g_pre_mix), vec(g_post_mix), vec(g_pre_ff), vec(g_post_ff)
    g_qa, g_kva, g_grp = vec(g_qa), vec(g_kva), vec(g_grp)
    cosq, sinq, cosk, sink = _rope_tables()
    ones_t = jnp.ones((PAST, LANE), F32)
    zeros_t = jnp.zeros((PAST, LANE), F32)
    bias = _na_bias(rpb)
    kc = cache_na_k.reshape(N_LAT_B, DEPTH, PAST, NA_W)
    vc = cache_na_v.reshape(N_LAT_B, DEPTH, PAST, NA_W)
    c_kr = jnp.pad(cache_mla_krope, ((0, 0), (0, 0), (0, 0), (NOPE, 2 * LANE - NOPE - ROPE)))

    st_k, st_v, st_ckv, st_kr = [], [], [], []
    for l in range(DEPTH):
        bu, q, k, v, qa, ckv, kr = _in_projection(l, y, mods, g_pre_mix, w_in_p, g_qa, g_kva)
        st_k.append(k[:T_CTX])
        st_v.append(v[:T_CTX])
        st_ckv.append(ckv[:T_CTX])
        st_kr.append(kr[:T_CTX, NOPE:NOPE + ROPE])

        att_ctx = _ctx_attention(l, q, k, v, qa, ckv, kr, wq, wk, wv2)
        na_lat = _na_latent(l, q, k.reshape(3, GROUP, NA_W), v.reshape(3, GROUP, NA_W), kc, vc, bias)
        qh, kh, vh = _mla_prep_latent(l, qa, ckv, kr, cosq, sinq, cosk, sink, wq, wqr, wk, wv2)
        kch, vch = _mla_prep_cache(l, cache_mla_ckv, c_kr, ones_t, zeros_t, wk, wv2)
        mla_lat = _mla_attention(qh, kh, kch, vh, vch)

        li = l // 2
        if l % 2 == 0:
            y, h = _merge(l, bu, att_ctx, na_lat, mla_lat, y, mods, w_conv, g_grp, w_out_b, g_post_mix, g_pre_ff)
            y = _dense_ffn(l, h, y, mods, ff1, ff3, ff2, g_post_ff)
        else:
            y, h, route = _merge(l, bu, att_ctx, na_lat, mla_lat, y, mods, w_conv, g_grp, w_out_b, g_post_mix, g_pre_ff,
                                 router=(wr_hi, wr_lo, b_r))
            z = _moe_experts(li, *_moe_plan(route), h, moe_w1, moe_w3, moe_w2)
            y = _moe_combine(l, z, route, y, mods, g_post_ff)

    y_prompt = y[:T_CTX].reshape(N_CTX_B, CTX_L, D)
    y_sample = y[T_CTX:].reshape(N_LAT_B, LAT_L, D)
    stack = lambda xs, shp: jnp.stack([a.reshape((N_CTX_B, CTX_L) + shp) for a in xs], axis=1)
    return (y_prompt, y_sample, stack(st_k, (NA_H, NA_D)), stack(st_v, (NA_H, NA_D)),
            stack(st_ckv, (KV_RANK,)), stack(st_kr, (ROPE,)))
```
